```python
import math
import jax, jax.numpy as jnp
from jax import lax
import numpy as np

D_MODEL = 2048
BATCH = 4
SEQ = 8192
DEPTH = 4
DEC_BATCH = 1
DEC_SEQ = 16384
PAST_LEN = 128

D_SSM = D_MODEL // 2
SSM_HEAD_DIM = 64
N_SSM_HEADS = D_SSM // SSM_HEAD_DIM
SSM_GROUPS = 2
HEADS_PER_GROUP = N_SSM_HEADS // SSM_GROUPS
D_STATE = 128
CONV_K = 5
CHUNK = 128
D_POOL = D_MODEL - D_SSM
POOL_WINDOWS = (2, 4, 8, 16)
POOL_GROUP_DIM = D_POOL // len(POOL_WINDOWS)
D_MIX = D_SSM + D_POOL
CONV_DIM = D_SSM + 2 * SSM_GROUPS * D_STATE
IN_DIM = D_SSM + CONV_DIM + 2 * N_SSM_HEADS + D_POOL
D_FF = 4 * D_MODEL
NORM_EPS = 1e-6

kernel_name = "bidir_ssd_pool_hybrid_encoder"


def rmsnorm(x, g):
    xf = x.astype(jnp.float32)
    y = xf * lax.rsqrt(jnp.mean(xf * xf, axis=-1, keepdims=True) + NORM_EPS)
    return (y * g.astype(jnp.float32)).astype(x.dtype)


def gated_group_rmsnorm(y, z, w):
    b, l, _ = y.shape
    v = (y.astype(jnp.float32) * jax.nn.silu(z.astype(jnp.float32))).reshape(b, l, SSM_GROUPS, D_SSM // SSM_GROUPS)
    v = v * lax.rsqrt(jnp.mean(v * v, axis=-1, keepdims=True) + NORM_EPS)
    return (v.reshape(b, l, D_SSM) * w.astype(jnp.float32)).astype(y.dtype)


def depthwise_conv_centred(u, w, bias):
    out = lax.conv_general_dilated(
        u, w[:, None, :].astype(u.dtype), window_strides=(1,),
        padding=[(CONV_K // 2, CONV_K // 2)],
        dimension_numbers=("NWC", "WIO", "NWC"),
        feature_group_count=u.shape[-1])
    return out + bias.astype(u.dtype)


def ssd_chunked(x, dt, a, bm, cm):
    bsz, l = x.shape[0], x.shape[1]
    nc = l // CHUNK
    xf = x.astype(jnp.float32).reshape(bsz, nc, CHUNK, SSM_GROUPS, HEADS_PER_GROUP, SSM_HEAD_DIM)
    dtf = dt.astype(jnp.float32).reshape(bsz, nc, CHUNK, SSM_GROUPS, HEADS_PER_GROUP)
    bf = bm.astype(jnp.float32).reshape(bsz, nc, CHUNK, SSM_GROUPS, D_STATE)
    cf = cm.astype(jnp.float32).reshape(bsz, nc, CHUNK, SSM_GROUPS, D_STATE)
    acum = jnp.cumsum(dtf * a.astype(jnp.float32), axis=2)

    seg = acum[:, :, :, None] - acum[:, :, None, :]
    causal = jnp.tril(jnp.ones((CHUNK, CHUNK), dtype=bool))[None, None, :, :, None, None]
    lmat = jnp.exp(jnp.where(causal, seg, -jnp.inf))
    scores = jnp.einsum("bcqgn,bcsgn->bcqsg", cf, bf)
    wts = scores[..., None] * lmat * dtf[:, :, None]
    y_diag = jnp.einsum("bcqsgr,bcsgrp->bcqgrp", wts, xf)

    decay_states = jnp.exp(acum[:, :, -1:] - acum) * dtf
    states = jnp.einsum("bcsgn,bcsgr,bcsgrp->bcgrpn", bf, decay_states, xf)
    chunk_decay = jnp.exp(acum[:, :, -1])

    def step(h, inp):
        st, dec = inp
        return dec[..., None, None] * h + st, h

    h0 = jnp.zeros((bsz, SSM_GROUPS, HEADS_PER_GROUP, SSM_HEAD_DIM, D_STATE), jnp.float32)
    _, prev = lax.scan(step, h0, (jnp.moveaxis(states, 1, 0), jnp.moveaxis(chunk_decay, 1, 0)))
    prev = jnp.moveaxis(prev, 0, 1)

    y_off = jnp.einsum("bcqgn,bcgrpn,bcqgr->bcqgrp", cf, prev, jnp.exp(acum))
    return (y_diag + y_off).reshape(bsz, l, SSM_GROUPS, HEADS_PER_GROUP, SSM_HEAD_DIM)


def multi_scale_pool(u, pool_w, pool_scale):
    bsz, l, _ = u.shape
    uf = u.astype(jnp.float32)
    cs = jnp.concatenate([jnp.zeros((bsz, 1, D_POOL), jnp.float32), jnp.cumsum(uf, axis=1)], axis=1)
    t = jnp.arange(l)
    pooled = []
    for gi, w in enumerate(POOL_WINDOWS):
        cs_g = cs[..., gi * POOL_GROUP_DIM:(gi + 1) * POOL_GROUP_DIM]
        hi = jnp.minimum(t + w // 2, l)
        lo = jnp.maximum(t - w // 2, 0)
        s = jnp.take(cs_g, hi, axis=1) - jnp.take(cs_g, lo, axis=1)
        pooled.append(s / (hi - lo).astype(jnp.float32)[None, :, None])
    d = jnp.concatenate(pooled, axis=-1) - uf
    d = d.reshape(bsz, l, len(POOL_WINDOWS), POOL_GROUP_DIM)
    out = jnp.einsum("blgc,gcd->blgd", d, pool_w.astype(jnp.float32)).reshape(bsz, l, D_POOL)
    return (out * pool_scale.astype(jnp.float32)).astype(u.dtype)


def hybrid_layer(x, g_mix_pre, g_mix_post, g_mlp_pre, g_mlp_post, w_in, conv_w, conv_b,
                 dt_bias, a_log, d_skip, ssm_norm_w, pool_w, pool_scale, w_out,
                 w_mlp_in, w_mlp_out):
    bsz, l, _ = x.shape
    h = rmsnorm(x, g_mix_pre)
    proj = jnp.einsum("bld,de->ble", h, w_in)
    z = proj[..., :D_SSM]
    xbc = proj[..., D_SSM:D_SSM + CONV_DIM]
    dt_raw = proj[..., D_SSM + CONV_DIM:D_SSM + CONV_DIM + 2 * N_SSM_HEADS].astype(jnp.float32)
    u_pool = proj[..., D_SSM + CONV_DIM + 2 * N_SSM_HEADS:]

    xbc = jax.nn.silu(depthwise_conv_centred(xbc, conv_w, conv_b))
    xs = xbc[..., :D_SSM].reshape(bsz, l, SSM_GROUPS, HEADS_PER_GROUP, SSM_HEAD_DIM)
    bm = xbc[..., D_SSM:D_SSM + SSM_GROUPS * D_STATE].reshape(bsz, l, SSM_GROUPS, D_STATE)
    cm = xbc[..., D_SSM + SSM_GROUPS * D_STATE:].reshape(bsz, l, SSM_GROUPS, D_STATE)
    dtb = dt_bias.astype(jnp.float32)
    dt_f = jax.nn.softplus(dt_raw[..., :N_SSM_HEADS] + dtb[0]).reshape(bsz, l, SSM_GROUPS, HEADS_PER_GROUP)
    dt_b = jax.nn.softplus(dt_raw[..., N_SSM_HEADS:] + dtb[1]).reshape(bsz, l, SSM_GROUPS, HEADS_PER_GROUP)
    a = -jnp.exp(a_log.astype(jnp.float32)).reshape(2, SSM_GROUPS, HEADS_PER_GROUP)
    y_fwd = ssd_chunked(xs, dt_f, a[0], bm, cm)
    y_bwd = jnp.flip(ssd_chunked(jnp.flip(xs, 1), jnp.flip(dt_b, 1), a[1],
                                 jnp.flip(bm, 1), jnp.flip(cm, 1)), 1)
    d = d_skip.astype(jnp.float32).reshape(SSM_GROUPS, HEADS_PER_GROUP)[:, :, None]
    y_ssm = (y_fwd + y_bwd + d * xs.astype(jnp.float32)).reshape(bsz, l, D_SSM).astype(x.dtype)
    y_ssm = gated_group_rmsnorm(y_ssm, z, ssm_norm_w)

    y_pool = multi_scale_pool(u_pool, pool_w, pool_scale)

    mix = jnp.einsum("ble,ed->bld", jnp.concatenate([y_ssm, y_pool], axis=-1), w_out)
    x = x + rmsnorm(mix, g_mix_post)

    h = rmsnorm(x, g_mlp_pre)
    m = jnp.square(jax.nn.relu(jnp.einsum("bld,df->blf", h, w_mlp_in)))
    m = jnp.einsum("blf,fd->bld", m, w_mlp_out)
    return x + rmsnorm(m, g_mlp_post)


def trunk(x, g_mix_pre, g_mix_post, g_mlp_pre, g_mlp_post, w_in, conv_w, conv_b,
          dt_bias, a_log, d_skip, ssm_norm_w, pool_w, pool_scale, w_out,
          w_mlp_in, w_mlp_out):
    for i in range(DEPTH):
        x = hybrid_layer(x, g_mix_pre[i], g_mix_post[i], g_mlp_pre[i], g_mlp_post[i],
                         w_in[i], conv_w[i], conv_b[i], dt_bias[i], a_log[i], d_skip[i],
                         ssm_norm_w[i], pool_w[i], pool_scale[i], w_out[i],
                         w_mlp_in[i], w_mlp_out[i])
    return x


def setup_inputs(seed: int = 0) -> dict:
    key = jax.random.key(seed)
    ks = jax.random.split(key, 20)

    def nrm(k, shape, scale):
        return jax.random.normal(k, shape, jnp.float32) * scale

    def gain(k, shape):
        return 1.0 + nrm(k, shape, 0.1)

    dt0 = jnp.exp(jax.random.uniform(ks[9], (DEPTH, 2, N_SSM_HEADS), jnp.float32,
                                     minval=math.log(1e-3), maxval=math.log(1e-1)))
    dt_bias = dt0 + jnp.log(-jnp.expm1(-dt0))
    a_log = jnp.log(jax.random.uniform(ks[10], (DEPTH, 2, N_SSM_HEADS), jnp.float32,
                                       minval=1.0, maxval=16.0))
    return {
        "x_prompt": nrm(ks[0], (BATCH, SEQ, D_MODEL), 1.0),
        "x_sample": nrm(ks[1], (DEC_BATCH, DEC_SEQ, D_MODEL), 1.0),
        "g_mix_pre": gain(ks[2], (DEPTH, D_MODEL)),
        "g_mix_post": gain(ks[3], (DEPTH, D_MODEL)),
        "g_mlp_pre": gain(ks[4], (DEPTH, D_MODEL)),
        "g_mlp_post": gain(ks[5], (DEPTH, D_MODEL)),
        "w_in": nrm(ks[6], (DEPTH, D_MODEL, IN_DIM), D_MODEL ** -0.5),
        "conv_w": nrm(ks[7], (DEPTH, CONV_K, CONV_DIM), CONV_K ** -0.5),
        "conv_b": nrm(ks[8], (DEPTH, CONV_DIM), 0.02),
        "dt_bias": dt_bias,
        "a_log": a_log,
        "d_skip": gain(ks[11], (DEPTH, N_SSM_HEADS)),
        "ssm_norm_w": gain(ks[12], (DEPTH, D_SSM)),
        "pool_w": nrm(ks[13], (DEPTH, len(POOL_WINDOWS), POOL_GROUP_DIM, POOL_GROUP_DIM), POOL_GROUP_DIM ** -0.5),
        "pool_scale": gain(ks[14], (DEPTH, D_POOL)),
        "w_out": nrm(ks[15], (DEPTH, D_MIX, D_MODEL), D_MIX ** -0.5),
        "w_mlp_in": nrm(ks[16], (DEPTH, D_MODEL, D_FF), D_MODEL ** -0.5),
        "w_mlp_out": nrm(ks[17], (DEPTH, D_FF, D_MODEL), D_FF ** -0.5),
    }


def reference(x_prompt, x_sample, g_mix_pre, g_mix_post, g_mlp_pre, g_mlp_post, w_in,
              conv_w, conv_b, dt_bias, a_log, d_skip, ssm_norm_w, pool_w, pool_scale,
              w_out, w_mlp_in, w_mlp_out):
    y_prompt = trunk(x_prompt, g_mix_pre, g_mix_post, g_mlp_pre, g_mlp_post, w_in, conv_w,
                     conv_b, dt_bias, a_log, d_skip, ssm_norm_w, pool_w, pool_scale,
                     w_out, w_mlp_in, w_mlp_out)
    y_sample = trunk(x_sample, g_mix_pre, g_mix_post, g_mlp_pre, g_mlp_post, w_in, conv_w,
                     conv_b, dt_bias, a_log, d_skip, ssm_norm_w, pool_w, pool_scale,
                     w_out, w_mlp_in, w_mlp_out)
    return (y_prompt, y_sample)
```

```python
import functools

import numpy as np
import jax
import jax.numpy as jnp
from jax import lax
from jax.experimental import pallas as pl
from jax.experimental.pallas import tpu as pltpu

D_MODEL = 2048
D_SSM = 1024
HEAD_DIM = 64
N_HEADS = 16
GROUPS = 2
HEADS_PER_GROUP = N_HEADS // GROUPS
GROUP_W = HEADS_PER_GROUP * HEAD_DIM
D_STATE = 128
CONV_K = 5
CHUNK = 128
D_POOL = 1024
POOL_WINDOWS = (2, 4, 8, 16)
POOL_GROUP_DIM = D_POOL // len(POOL_WINDOWS)
CONV_DIM = D_SSM + 2 * GROUPS * D_STATE
D_FF = 4 * D_MODEL
NORM_EPS = 1e-6

LANES = 128
SUBLANES = 8
HALO = SUBLANES
DT_PAD = LANES
POOL_K = 2 * CHUNK
NEG_BIG = -1e30
VMEM_LIMIT = 56 * 1024 * 1024

F32 = jnp.float32
BF16 = jnp.bfloat16


def _dot(a, b):
    return jnp.dot(a, b, preferred_element_type=F32)


def _dot_nt(a, b):
    return lax.dot_general(a, b, (((1,), (1,)), ((), ())), preferred_element_type=F32)


def _split3(v):
    hi = v.astype(BF16)
    r1 = v - hi.astype(F32)
    lo = r1.astype(BF16)
    lo2 = (r1 - lo.astype(F32)).astype(BF16)
    return hi, lo, lo2


def _exact_rdot(v, sel):
    hi, lo, lo2 = _split3(v)
    return _dot(hi, sel) + _dot(lo, sel) + _dot(lo2, sel)


def _exact_ldot(sel, v):
    hi, lo, lo2 = _split3(v)
    return _dot(sel, hi) + _dot(sel, lo) + _dot(sel, lo2)


def _silu(v):
    return v * (1.0 / (1.0 + jnp.exp(-v)))


def _softplus(v):
    return jnp.maximum(v, 0.0) + jnp.log1p(jnp.exp(-jnp.abs(v)))


def _rms(v):
    return v * lax.rsqrt(jnp.mean(v * v, axis=-1, keepdims=True) + NORM_EPS)


def _in_proj_kernel(x_ref, g_ref, wz_ref, wxbc_ref, wdt_ref, wu_ref,
                    z_ref, xbc_ref, dt_ref, u_ref):
    h = (_rms(x_ref[...]) * g_ref[...]).astype(BF16)
    z_ref[...] = _dot(h, wz_ref[...])
    xbc_ref[...] = _dot(h, wxbc_ref[...])
    dt_ref[...] = _dot(h, wdt_ref[...])
    u_ref[...] = _dot(h, wu_ref[...])


def _in_proj(x, g, wz, wxbc, wdt, wu, *, tm):
    t = x.shape[0]
    row = lambda i: (i, 0)
    fixed = lambda i: (0, 0)
    const = functools.partial(pl.BlockSpec, index_map=fixed, pipeline_mode=pl.Buffered(1))
    return pl.pallas_call(
        _in_proj_kernel,
        grid=(t // tm,),
        in_specs=[
            pl.BlockSpec((tm, D_MODEL), row),
            const((1, D_MODEL)),
            const((D_MODEL, D_SSM)),
            const((D_MODEL, CONV_DIM)),
            const((D_MODEL, DT_PAD)),
            const((D_MODEL, D_POOL)),
        ],
        out_specs=[
            pl.BlockSpec((tm, D_SSM), row),
            pl.BlockSpec((tm, CONV_DIM), row),
            pl.BlockSpec((tm, DT_PAD), row),
            pl.BlockSpec((tm, D_POOL), row),
        ],
        out_shape=[
            jax.ShapeDtypeStruct((t, D_SSM), F32),
            jax.ShapeDtypeStruct((t, CONV_DIM), F32),
            jax.ShapeDtypeStruct((t, DT_PAD), F32),
            jax.ShapeDtypeStruct((t, D_POOL), F32),
        ],
        compiler_params=pltpu.CompilerParams(
            dimension_semantics=("arbitrary",), vmem_limit_bytes=VMEM_LIMIT),
        name="in_proj",
    )(x, g, wz, wxbc, wdt, wu)


def _ssd_chunk(xc, dtr, dtb, a_row, cum_sel, colsel, headsel, h_ref, *, reverse):
    lane_off = N_HEADS if reverse else 0
    edge = 0 if reverse else CHUNK - 1
    xs = xc[:, :D_SSM]
    xs_b = xs.astype(BF16)

    dt = _softplus(dtr + dtb)
    acum = _exact_ldot(cum_sel, dt * a_row)
    g_t = (acum - jnp.log(dt)).T
    a_col = _exact_rdot(acum, colsel)
    a_edge = acum[edge:edge + 1, :]
    w_exp = _exact_rdot(jnp.exp(a_edge - acum) * dt, headsel)
    e_exp = _exact_rdot(jnp.exp(acum), headsel)
    xw_b = (w_exp * xs).astype(BF16)

    q_idx = lax.broadcasted_iota(jnp.int32, (CHUNK, CHUNK), 0)
    s_idx = lax.broadcasted_iota(jnp.int32, (CHUNK, CHUNK), 1)
    allowed = (s_idx >= q_idx) if reverse else (s_idx <= q_idx)
    low_half = lax.broadcasted_iota(jnp.int32, (CHUNK, 2 * HEAD_DIM), 1) < HEAD_DIM

    pieces = []
    for g in range(GROUPS):
        b_g = xc[:, D_SSM + g * D_STATE:D_SSM + (g + 1) * D_STATE]
        c_g = xc[:, D_SSM + GROUPS * D_STATE + g * D_STATE:
                 D_SSM + GROUPS * D_STATE + (g + 1) * D_STATE].astype(BF16)
        scores = _dot_nt(c_g, b_g.astype(BF16))
        h_prev = h_ref[g]
        y_off = _dot(c_g, h_prev.astype(BF16))
        for jp in range(HEADS_PER_GROUP // 2):
            h0 = g * HEADS_PER_GROUP + 2 * jp
            res = []
            for hh in (h0, h0 + 1):
                seg = (a_col[:, hh * CHUNK:(hh + 1) * CHUNK]
                       - g_t[lane_off + hh:lane_off + hh + 1, :])
                wmat = scores * jnp.exp(jnp.where(allowed, seg, NEG_BIG))
                res.append(_dot(wmat.astype(BF16), xs_b[:, h0 * HEAD_DIM:(h0 + 2) * HEAD_DIM]))
            y_diag = jnp.where(low_half, res[0], res[1])
            lo = h0 * HEAD_DIM
            pieces.append(y_diag + y_off[:, 2 * jp * HEAD_DIM:(2 * jp + 2) * HEAD_DIM]
                          * e_exp[:, lo:lo + 2 * HEAD_DIM])
        states = _dot(b_g.T.astype(BF16), xw_b[:, g * GROUP_W:(g + 1) * GROUP_W])
        decay = e_exp[edge:edge + 1, g * GROUP_W:(g + 1) * GROUP_W]
        h_ref[g] = decay * h_prev + states
    return jnp.concatenate(pieces, axis=-1)


def _dt_consts(alog_ref):
    lane = lax.broadcasted_iota(jnp.int32, (1, DT_PAD), 1)
    return jnp.where(lane < 2 * N_HEADS, -jnp.exp(alog_ref[...]), 0.0)


def _ssd_fwd_kernel(xbc_ref, prev_ref, next_ref, dt_ref, cw_ref, cb_ref, dtb_ref, alog_ref,
                    dskip_ref, tril_ref, colsel_ref, headsel_ref,
                    xc_ref, yf_ref, ext_ref, h_ref, *, tm):
    i = pl.program_id(1)
    last = pl.num_programs(1) - 1

    @pl.when(i == 0)
    def _():
        h_ref[...] = jnp.zeros_like(h_ref)

    ext_ref[0:HALO, :] = jnp.where(i > 0, prev_ref[...], 0.0)
    ext_ref[HALO:HALO + tm, :] = xbc_ref[...]
    ext_ref[HALO + tm:, :] = jnp.where(i < last, next_ref[...], 0.0)

    a_row = _dt_consts(alog_ref)
    dtb = dtb_ref[...]
    for c in range(tm // CHUNK):
        r0 = c * CHUNK
        acc = cb_ref[...] + cw_ref[0:1, :] * ext_ref[r0 + HALO - 2:r0 + HALO - 2 + CHUNK, :]
        for k in range(1, CONV_K):
            off = r0 + HALO - CONV_K // 2 + k
            acc = acc + cw_ref[k:k + 1, :] * ext_ref[off:off + CHUNK, :]
        xc = _silu(acc)
        xc_ref[r0:r0 + CHUNK, :] = xc
        y = _ssd_chunk(xc, dt_ref[r0:r0 + CHUNK, :], dtb, a_row, tril_ref[...],
                       colsel_ref[...], headsel_ref[...], h_ref, reverse=False)
        yf_ref[r0:r0 + CHUNK, :] = y + dskip_ref[...] * xc[:, :D_SSM]


def _ssd_fwd(xbc, dt, cw, cb, dtb, alog, dskip, tril, colsel, headsel, *, nb, seq, tm):
    nt = seq // tm
    hb = tm // HALO
    nblk8 = nb * seq // HALO
    row = lambda b, i: (b * nt + i, 0)
    prev = lambda b, i: (jnp.maximum((b * nt + i) * hb - 1, 0), 0)
    nxt = lambda b, i: (jnp.minimum((b * nt + i + 1) * hb, nblk8 - 1), 0)
    fixed = lambda b, i: (0, 0)
    const = functools.partial(pl.BlockSpec, index_map=fixed, pipeline_mode=pl.Buffered(1))
    t = nb * seq
    return pl.pallas_call(
        functools.partial(_ssd_fwd_kernel, tm=tm),
        grid=(nb, nt),
        in_specs=[
            pl.BlockSpec((tm, CONV_DIM), row),
            pl.BlockSpec((HALO, CONV_DIM), prev),
            pl.BlockSpec((HALO, CONV_DIM), nxt),
            pl.BlockSpec((tm, DT_PAD), row),
            const((SUBLANES, CONV_DIM)),
            const((1, CONV_DIM)),
            const((1, DT_PAD)),
            const((1, DT_PAD)),
            const((1, D_SSM)),
            const((CHUNK, CHUNK)),
            const((DT_PAD, N_HEADS * CHUNK)),
            const((DT_PAD, D_SSM)),
        ],
        out_specs=[
            pl.BlockSpec((tm, CONV_DIM), row),
            pl.BlockSpec((tm, D_SSM), row),
        ],
        out_shape=[
            jax.ShapeDtypeStruct((t, CONV_DIM), F32),
            jax.ShapeDtypeStruct((t, D_SSM), F32),
        ],
        scratch_shapes=[
            pltpu.VMEM((tm + 2 * HALO, CONV_DIM), F32),
            pltpu.VMEM((GROUPS, D_STATE, GROUP_W), F32),
        ],
        compiler_params=pltpu.CompilerParams(
            dimension_semantics=("arbitrary", "arbitrary"), vmem_limit_bytes=VMEM_LIMIT),
        name="ssd_fwd",
    )(xbc, xbc, xbc, dt, cw, cb, dtb, alog, dskip, tril, colsel, headsel)


def _ssd_bwd_kernel(xc_ref, dt_ref, yf_ref, z_ref, u_ref, uprev_ref, unext_ref, x_ref,
                    dtb_ref, alog_ref, triu_ref, colsel_ref, headsel_ref, normw_ref,
                    dmat_ref, poolw_ref, pscale_ref, wout_ref, gpost_ref,
                    o_ref, y_ref, extu_ref, mixin_ref, h_ref, *, tm):
    j = pl.program_id(1)
    nt = pl.num_programs(1)
    tile = nt - 1 - j

    @pl.when(j == 0)
    def _():
        h_ref[...] = jnp.zeros_like(h_ref)

    a_row = _dt_consts(alog_ref)
    dtb = dtb_ref[...]
    nchunk = tm // CHUNK
    for c in reversed(range(nchunk)):
        r0 = c * CHUNK
        y = _ssd_chunk(xc_ref[r0:r0 + CHUNK, :], dt_ref[r0:r0 + CHUNK, :], dtb, a_row,
                       triu_ref[...], colsel_ref[...], headsel_ref[...], h_ref, reverse=True)
        y_ref[r0:r0 + CHUNK, :] = y + yf_ref[r0:r0 + CHUNK, :]

    v = y_ref[...] * _silu(z_ref[...])
    for g in range(GROUPS):
        lo = g * GROUP_W
        mixin_ref[:, lo:lo + GROUP_W] = (_rms(v[:, lo:lo + GROUP_W])
                                         * normw_ref[:, lo:lo + GROUP_W]).astype(BF16)

    extu_ref[0:HALO, :] = jnp.where(tile > 0, uprev_ref[...], 0.0)
    extu_ref[HALO:HALO + tm, :] = u_ref[...]
    extu_ref[HALO + tm:HALO + tm + HALO, :] = jnp.where(tile < nt - 1, unext_ref[...], 0.0)
    extu_ref[HALO + tm + HALO:, :] = jnp.zeros((POOL_K - CHUNK - 2 * HALO, D_POOL), F32)
    for c in range(nchunk):
        r0 = c * CHUNK
        first = jnp.logical_and(tile == 0, c == 0)
        final = jnp.logical_and(tile == nt - 1, c == nchunk - 1)
        variant = jnp.where(first, 0, jnp.where(final, 2, 1))
        for gi in range(len(POOL_WINDOWS)):
            lo = gi * POOL_GROUP_DIM
            band = dmat_ref[variant * len(POOL_WINDOWS) + gi]
            src = extu_ref[r0:r0 + POOL_K, lo:lo + POOL_GROUP_DIM].astype(BF16)
            d = _dot(band, src).astype(BF16)
            pooled = _dot(d, poolw_ref[gi]) * pscale_ref[:, lo:lo + POOL_GROUP_DIM]
            mixin_ref[r0:r0 + CHUNK, D_SSM + lo:D_SSM + lo + POOL_GROUP_DIM] = pooled.astype(BF16)

    mix = _dot(mixin_ref[...], wout_ref[...])
    o_ref[...] = x_ref[...] + _rms(mix) * gpost_ref[...]


def _ssd_bwd(xc, dt, yf, z, u, x, dtb, alog, triu, colsel, headsel, normw, dmat, poolw, pscale,
             wout, gpost, *, nb, seq, tm):
    nt = seq // tm
    hb = tm // HALO
    nblk8 = nb * seq // HALO
    row = lambda b, j: (b * nt + nt - 1 - j, 0)
    prev = lambda b, j: (jnp.maximum((b * nt + nt - 1 - j) * hb - 1, 0), 0)
    nxt = lambda b, j: (jnp.minimum((b * nt + nt - j) * hb, nblk8 - 1), 0)
    const2 = functools.partial(pl.BlockSpec, index_map=lambda b, j: (0, 0),
                               pipeline_mode=pl.Buffered(1))
    const3 = functools.partial(pl.BlockSpec, index_map=lambda b, j: (0, 0, 0),
                               pipeline_mode=pl.Buffered(1))
    t = nb * seq
    nvar = 3 * len(POOL_WINDOWS)
    return pl.pallas_call(
        functools.partial(_ssd_bwd_kernel, tm=tm),
        grid=(nb, nt),
        in_specs=[
            pl.BlockSpec((tm, CONV_DIM), row),
            pl.BlockSpec((tm, DT_PAD), row),
            pl.BlockSpec((tm, D_SSM), row),
            pl.BlockSpec((tm, D_SSM), row),
            pl.BlockSpec((tm, D_POOL), row),
            pl.BlockSpec((HALO, D_POOL), prev),
            pl.BlockSpec((HALO, D_POOL), nxt),
            pl.BlockSpec((tm, D_MODEL), row),
            const2((1, DT_PAD)),
            const2((1, DT_PAD)),
            const2((CHUNK, CHUNK)),
            const2((DT_PAD, N_HEADS * CHUNK)),
            const2((DT_PAD, D_SSM)),
            const2((1, D_SSM)),
            const3((nvar, CHUNK, POOL_K)),
            const3((len(POOL_WINDOWS), POOL_GROUP_DIM, POOL_GROUP_DIM)),
            const2((1, D_POOL)),
            const2((D_SSM + D_POOL, D_MODEL)),
            const2((1, D_MODEL)),
        ],
        out_specs=pl.BlockSpec((tm, D_MODEL), row),
        out_shape=jax.ShapeDtypeStruct((t, D_MODEL), F32),
        scratch_shapes=[
            pltpu.VMEM((tm, D_SSM), F32),
            pltpu.VMEM((tm + CHUNK, D_POOL), F32),
            pltpu.VMEM((tm, D_SSM + D_POOL), BF16),
            pltpu.VMEM((GROUPS, D_STATE, GROUP_W), F32),
        ],
        compiler_params=pltpu.CompilerParams(
            dimension_semantics=("arbitrary", "arbitrary"), vmem_limit_bytes=VMEM_LIMIT),
        name="ssd_bwd_mix",
    )(xc, dt, yf, z, u, u, u, x, dtb, alog, triu, colsel, headsel, normw, dmat, poolw, pscale,
      wout, gpost)


def _mlp_kernel(x_ref, gpre_ref, w1_ref, w2_ref, gpost_ref, o_ref, h_ref, acc_ref):
    f = pl.program_id(1)

    @pl.when(f == 0)
    def _():
        h_ref[...] = (_rms(x_ref[...]) * gpre_ref[...]).astype(BF16)
        acc_ref[...] = jnp.zeros_like(acc_ref)

    a = jnp.maximum(_dot(h_ref[...], w1_ref[...]), 0.0)
    acc_ref[...] += _dot((a * a).astype(BF16), w2_ref[...])

    @pl.when(f == pl.num_programs(1) - 1)
    def _():
        o_ref[...] = x_ref[...] + _rms(acc_ref[...]) * gpost_ref[...]


def _mlp(x, gpre, w1, w2, gpost, *, tm, tf):
    t = x.shape[0]
    row = lambda i, f: (i, 0)
    fixed = lambda i, f: (0, 0)
    const = functools.partial(pl.BlockSpec, index_map=fixed, pipeline_mode=pl.Buffered(1))
    return pl.pallas_call(
        _mlp_kernel,
        grid=(t // tm, D_FF // tf),
        in_specs=[
            pl.BlockSpec((tm, D_MODEL), row),
            const((1, D_MODEL)),
            pl.BlockSpec((D_MODEL, tf), lambda i, f: (0, f)),
            pl.BlockSpec((tf, D_MODEL), lambda i, f: (f, 0)),
            const((1, D_MODEL)),
        ],
        out_specs=pl.BlockSpec((tm, D_MODEL), row),
        out_shape=jax.ShapeDtypeStruct((t, D_MODEL), F32),
        scratch_shapes=[
            pltpu.VMEM((tm, D_MODEL), BF16),
            pltpu.VMEM((tm, D_MODEL), F32),
        ],
        compiler_params=pltpu.CompilerParams(
            dimension_semantics=("arbitrary", "arbitrary"), vmem_limit_bytes=VMEM_LIMIT),
        name="mlp",
    )(x, gpre, w1, w2, gpost)


def _selectors():
    q = np.arange(CHUNK)[:, None]
    s = np.arange(CHUNK)[None, :]
    tril = (s <= q).astype(np.float32)
    triu = (s >= q).astype(np.float32)
    colsel = np.zeros((2, DT_PAD, N_HEADS * CHUNK), np.float32)
    headsel = np.zeros((2, DT_PAD, D_SSM), np.float32)
    for d in range(2):
        for h in range(N_HEADS):
            colsel[d, d * N_HEADS + h, h * CHUNK:(h + 1) * CHUNK] = 1.0
            headsel[d, d * N_HEADS + h, h * HEAD_DIM:(h + 1) * HEAD_DIM] = 1.0
    as_bf16 = lambda a: jnp.asarray(a, dtype=BF16)
    return as_bf16(tril), as_bf16(triu), as_bf16(colsel), as_bf16(headsel)


def _pool_bands(seq):
    out = np.zeros((3, len(POOL_WINDOWS), CHUNK, POOL_K), np.float32)
    for v, start in enumerate((0, CHUNK, seq - CHUNK)):
        t = start + np.arange(CHUNK)[:, None]
        src = start + np.arange(POOL_K)[None, :] - HALO
        for gi, w in enumerate(POOL_WINDOWS):
            hi = np.minimum(t + w // 2, seq)
            lo = np.maximum(t - w // 2, 0)
            inside = (src >= lo) & (src < hi)
            out[v, gi] = inside / (hi - lo).astype(np.float32) - (src == t)
    return jnp.asarray(out.reshape(3 * len(POOL_WINDOWS), CHUNK, POOL_K), dtype=BF16)


def _layer_params(i, g_mix_pre, g_mix_post, g_mlp_pre, g_mlp_post, w_in, conv_w, conv_b,
                  dt_bias, a_log, d_skip, ssm_norm_w, pool_w, pool_scale, w_out,
                  w_mlp_in, w_mlp_out):
    row = lambda v: v.reshape(1, -1).astype(F32)
    pad_lanes = lambda v: jnp.pad(row(v), ((0, 0), (0, DT_PAD - 2 * N_HEADS)))
    o_dt = D_SSM + CONV_DIM
    w = w_in[i]
    return dict(
        g_mix_pre=row(g_mix_pre[i]), g_mix_post=row(g_mix_post[i]),
        g_mlp_pre=row(g_mlp_pre[i]), g_mlp_post=row(g_mlp_post[i]),
        wz=w[:, :D_SSM].astype(BF16),
        wxbc=w[:, D_SSM:o_dt].astype(BF16),
        wdt=jnp.pad(w[:, o_dt:o_dt + 2 * N_HEADS],
                    ((0, 0), (0, DT_PAD - 2 * N_HEADS))).astype(BF16),
        wu=w[:, o_dt + 2 * N_HEADS:].astype(BF16),
        cw=jnp.pad(conv_w[i].astype(F32), ((0, SUBLANES - CONV_K), (0, 0))),
        cb=row(conv_b[i]),
        dtb=pad_lanes(dt_bias[i]), alog=pad_lanes(a_log[i]),
        dskip=row(jnp.repeat(d_skip[i], HEAD_DIM)),
        normw=row(ssm_norm_w[i]),
        poolw=pool_w[i].astype(BF16), pscale=row(pool_scale[i]),
        wout=w_out[i].astype(BF16),
        w1=w_mlp_in[i].astype(BF16), w2=w_mlp_out[i].astype(BF16),
    )


def _trunk(x, layers, consts, *, tm_proj, tm_ssd, tm_mlp, tf):
    nb, seq, _ = x.shape
    tril, triu, colsel, headsel = consts
    dmat = _pool_bands(seq)
    x = x.reshape(nb * seq, D_MODEL)
    for p in layers:
        z, xbc, dt, u = _in_proj(x, p["g_mix_pre"], p["wz"], p["wxbc"], p["wdt"], p["wu"],
                                 tm=tm_proj)
        xc, yf = _ssd_fwd(xbc, dt, p["cw"], p["cb"], p["dtb"], p["alog"], p["dskip"],
                          tril, colsel[0], headsel[0], nb=nb, seq=seq, tm=tm_ssd)
        x = _ssd_bwd(xc, dt, yf, z, u, x, p["dtb"], p["alog"], triu, colsel[1], headsel[1],
                     p["normw"], dmat, p["poolw"], p["pscale"], p["wout"], p["g_mix_post"],
                     nb=nb, seq=seq, tm=tm_ssd)
        x = _mlp(x, p["g_mlp_pre"], p["w1"], p["w2"], p["g_mlp_post"], tm=tm_mlp, tf=tf)
    return x.reshape(nb, seq, D_MODEL)


def kernel(x_prompt, x_sample, g_mix_pre, g_mix_post, g_mlp_pre, g_mlp_post, w_in, conv_w, conv_b, dt_bias, a_log, d_skip, ssm_norm_w, pool_w, pool_scale, w_out, w_mlp_in, w_mlp_out):
    depth = w_in.shape[0]
    layers = [_layer_params(i, g_mix_pre, g_mix_post, g_mlp_pre, g_mlp_post, w_in, conv_w,
                            conv_b, dt_bias, a_log, d_skip, ssm_norm_w, pool_w, pool_scale,
                            w_out, w_mlp_in, w_mlp_out) for i in range(depth)]
    consts = _selectors()
    run = functools.partial(_trunk, layers=layers, consts=consts,
                            tm_proj=512, tm_ssd=256, tm_mlp=512, tf=1024)
    return (run(x_prompt), run(x_sample))
```

```python
import functools

import numpy as np
import jax
import jax.numpy as jnp
from jax import lax
from jax.experimental import pallas as pl
from jax.experimental.pallas import tpu as pltpu

D_MODEL = 2048
D_SSM = 1024
HEAD_DIM = 64
N_HEADS = 16
GROUPS = 2
HEADS_PER_GROUP = N_HEADS // GROUPS
GROUP_W = HEADS_PER_GROUP * HEAD_DIM
D_STATE = 128
CONV_K = 5
CHUNK = 128
D_POOL = 1024
POOL_WINDOWS = (2, 4, 8, 16)
POOL_GROUP_DIM = D_POOL // len(POOL_WINDOWS)
CONV_DIM = D_SSM + 2 * GROUPS * D_STATE
D_FF = 4 * D_MODEL
NORM_EPS = 1e-6

LANES = 128
SUBLANES = 8
HALO = SUBLANES
DT_PAD = LANES
DT_COLS = 2 * N_HEADS
DT_REP = 3
HEADS_PER_DOT = 4
POOL_K = 2 * CHUNK
MXU_N = 256
FWD_LAG = 2
RING = 4
NEG_BIG = -1e30
VMEM_LIMIT = 56 * 1024 * 1024

F32 = jnp.float32
BF16 = jnp.bfloat16


def _dot(a, b):
    return jnp.dot(a, b, preferred_element_type=F32)


def _dot_nt(a, b):
    return lax.dot_general(a, b, (((1,), (1,)), ((), ())), preferred_element_type=F32)


def _split3(v):
    hi = v.astype(BF16)
    r1 = v - hi.astype(F32)
    lo = r1.astype(BF16)
    lo2 = (r1 - lo.astype(F32)).astype(BF16)
    return hi, lo, lo2


def _lane_split(v):
    hi, lo, lo2 = _split3(v)
    lane = lax.broadcasted_iota(jnp.int32, v.shape, 1)
    return jnp.where(lane < DT_COLS, hi, jnp.where(lane < 2 * DT_COLS, lo, lo2))


def _exact_rdot(v, sel):
    return _dot(_lane_split(v), sel)


def _exact_cumsum(sel3, v):
    return _dot(sel3, jnp.concatenate(_split3(v), axis=0))


def _silu(v):
    return v * (1.0 / (1.0 + jnp.exp(-v)))


def _softplus(v):
    return jnp.maximum(v, 0.0) + jnp.log1p(jnp.exp(-jnp.abs(v)))


def _rms(v):
    return v * lax.rsqrt(jnp.mean(v * v, axis=-1, keepdims=True) + NORM_EPS)


def _ssd_chunk(xc, dtr, dtb, a_row, cum_sel, colsel, headsel, h_ref, *, reverse, fresh=None,
               fill=lambda: None):
    lane_off = N_HEADS if reverse else 0
    edge = 0 if reverse else CHUNK - 1
    xs = xc[:, :D_SSM]
    xs_b = xs.astype(BF16)

    dt = _softplus(dtr + dtb)
    acum = _exact_cumsum(cum_sel, dt * a_row)
    fill()
    g_t = (acum - jnp.log(dt)).T
    a_col = _exact_rdot(acum, colsel)
    a_edge = acum[edge:edge + 1, :]
    w_exp = _exact_rdot(jnp.exp(a_edge - acum) * dt, headsel)
    e_exp = _exact_rdot(jnp.exp(acum), headsel)
    fill()
    xw_b = (w_exp * xs).astype(BF16)

    q_idx = lax.broadcasted_iota(jnp.int32, (CHUNK, CHUNK), 0)
    s_idx = lax.broadcasted_iota(jnp.int32, (CHUNK, CHUNK), 1)
    allowed = (s_idx >= q_idx) if reverse else (s_idx <= q_idx)
    quad_w = HEADS_PER_DOT * HEAD_DIM
    head_of_lane = lax.broadcasted_iota(jnp.int32, (CHUNK, quad_w), 1) // HEAD_DIM

    pieces = []
    for g in range(GROUPS):
        b_g = xc[:, D_SSM + g * D_STATE:D_SSM + (g + 1) * D_STATE]
        c_g = xc[:, D_SSM + GROUPS * D_STATE + g * D_STATE:
                 D_SSM + GROUPS * D_STATE + (g + 1) * D_STATE].astype(BF16)
        scores = _dot_nt(c_g, b_g.astype(BF16))
        h_prev = h_ref[g]
        if fresh is not None:
            h_prev = jnp.where(fresh, 0.0, h_prev)
        y_off = _dot(c_g, h_prev.astype(BF16))
        for qd in range(HEADS_PER_GROUP // HEADS_PER_DOT):
            h0 = g * HEADS_PER_GROUP + HEADS_PER_DOT * qd
            lo = h0 * HEAD_DIM
            wmats = []
            for hh in range(h0, h0 + HEADS_PER_DOT):
                seg = (a_col[:, hh * CHUNK:(hh + 1) * CHUNK]
                       - g_t[lane_off + hh:lane_off + hh + 1, :])
                wmats.append((scores * jnp.exp(jnp.where(allowed, seg, NEG_BIG))).astype(BF16))
            x_quad = xs_b[:, lo:lo + quad_w]
            x_diag = jnp.concatenate(
                [jnp.where(head_of_lane == k, x_quad, jnp.zeros_like(x_quad))
                 for k in range(HEADS_PER_DOT)], axis=0)
            y_diag = _dot(jnp.concatenate(wmats, axis=1), x_diag)
            fill()
            pieces.append(y_diag + y_off[:, qd * quad_w:(qd + 1) * quad_w]
                          * e_exp[:, lo:lo + quad_w])
        states = _dot(b_g.T.astype(BF16), xw_b[:, g * GROUP_W:(g + 1) * GROUP_W])
        decay = e_exp[edge:edge + 1, g * GROUP_W:(g + 1) * GROUP_W]
        h_ref[g] = decay * h_prev + states
    return jnp.concatenate(pieces, axis=-1)


def _dt_consts(alog_ref):
    lane = lax.broadcasted_iota(jnp.int32, (1, DT_PAD), 1)
    return jnp.where(lane < DT_REP * DT_COLS, -jnp.exp(alog_ref[...]), 0.0)


def _proj_fwd_kernel(x0_ref, xn_ref, g_ref, wz_ref, wxbc_ref, wdt_ref, wu_ref, cw_ref, cb_ref,
                     dtb_ref, alog_ref, dskip_ref, tril_ref, colsel_ref, headsel_ref,
                     z_ref, u_ref, dt_ref, xc_ref, yf_ref,
                     xbc_ring, dt_ring, h_ref, hn_ref, hn_next_ref, *, tm, nt):
    i = pl.program_id(1)

    @pl.when(i == 0)
    def _():
        xbc_ring[...] = jnp.zeros_like(xbc_ring)
        dt_ring[...] = jnp.zeros_like(dt_ring)
        h_ref[...] = jnp.zeros_like(h_ref)
        hn_next_ref[...] = (_rms(x0_ref[...]) * g_ref[...]).astype(BF16)

    t = i - FWD_LAG
    cur = lax.rem(i + RING - FWD_LAG, RING)
    prv = lax.rem(i + RING - FWD_LAG - 1, RING)
    nxt = lax.rem(i + RING - FWD_LAG + 1, RING)
    head = jnp.where(t > 0, xbc_ring[prv, tm - HALO:tm, :], 0.0)
    tail = jnp.where(t < nt - 1, xbc_ring[nxt, 0:HALO, :], 0.0)
    nchunk = tm // CHUNK
    mid = CONV_K // 2
    rows = CHUNK + 2 * HALO
    slot = lax.rem(i, RING)
    a_row = _dt_consts(alog_ref)
    dtb = dtb_ref[...]
    dts = dt_ring[cur]
    hn_ref[...] = hn_next_ref[...]

    def conv(c):
        r0 = c * CHUNK
        blk = jnp.concatenate([
            head if c == 0 else xbc_ring[cur, r0 - HALO:r0, :],
            xbc_ring[cur, r0:r0 + CHUNK, :],
            tail if c == nchunk - 1 else xbc_ring[cur, r0 + CHUNK:r0 + CHUNK + HALO, :],
        ], axis=0)
        acc = cb_ref[...] + cw_ref[mid:mid + 1, :] * blk[HALO:HALO + CHUNK, :]
        for k in range(CONV_K):
            if k != mid:
                rolled = pltpu.roll(blk, rows - (HALO - mid + k), axis=0)
                acc = acc + cw_ref[k:k + 1, :] * rolled[0:CHUNK, :]
        xc_ref[r0:r0 + CHUNK, :] = _silu(acc)

    def scan(c):
        r0 = c * CHUNK
        xc = xc_ref[r0:r0 + CHUNK, :]
        y = _ssd_chunk(xc, dts[r0:r0 + CHUNK, :], dtb, a_row, tril_ref[...], colsel_ref[...],
                       headsel_ref[...], h_ref, reverse=False,
                       fresh=(t <= 0) if c == 0 else None, fill=fill)
        yf_ref[r0:r0 + CHUNK, :] = y + dskip_ref[...] * xc[:, :D_SSM]

    def proj_z(lo):
        z_ref[:, lo:lo + MXU_N] = _dot(hn_ref[...], wz_ref[:, lo:lo + MXU_N])

    def proj_u(lo):
        u_ref[:, lo:lo + MXU_N] = _dot(hn_ref[...], wu_ref[:, lo:lo + MXU_N])

    def proj_dt():
        dtr = _dot(hn_ref[...], wdt_ref[...])
        dt_ref[...] = dtr
        dt_ring[slot] = dtr

    def proj_xbc(lo):
        xbc_ring[slot, :, lo:lo + MXU_N] = _dot(hn_ref[...], wxbc_ref[:, lo:lo + MXU_N])

    pieces = ([functools.partial(proj_z, lo) for lo in range(0, D_SSM, MXU_N)]
              + [functools.partial(proj_u, lo) for lo in range(0, D_POOL, MXU_N)]
              + [proj_dt]
              + [functools.partial(proj_xbc, lo) for lo in range(0, CONV_DIM, MXU_N)])
    queue = iter(pieces)

    def fill(n=1):
        for _ in range(n):
            piece = next(queue, None)
            if piece is not None:
                piece()

    for c in range(nchunk):
        fill(2)
        conv(c)
    for c in range(nchunk):
        scan(c)
    hn_next_ref[...] = (_rms(xn_ref[...]) * g_ref[...]).astype(BF16)
    fill(len(pieces))


def _proj_fwd(x, g, wz, wxbc, wdt, wu, cw, cb, dtb, alog, dskip, tril, colsel, headsel,
              *, nb, seq, tm):
    nt = seq // tm
    proj_row = lambda b, i: (b * nt + jnp.minimum(i, nt - 1), 0)
    norm_row = lambda b, i: (b * nt + jnp.minimum(i + 1, nt - 1), 0)
    scan_row = lambda b, i: (b * nt + jnp.clip(i - FWD_LAG, 0, nt - 1), 0)
    const = functools.partial(pl.BlockSpec, index_map=lambda b, i: (0, 0),
                              pipeline_mode=pl.Buffered(1))
    t = nb * seq
    return pl.pallas_call(
        functools.partial(_proj_fwd_kernel, tm=tm, nt=nt),
        grid=(nb, nt + FWD_LAG),
        in_specs=[
            pl.BlockSpec((tm, D_MODEL), lambda b, i: (b * nt, 0)),
            pl.BlockSpec((tm, D_MODEL), norm_row),
            const((1, D_MODEL)),
            const((D_MODEL, D_SSM)),
            const((D_MODEL, CONV_DIM)),
            const((D_MODEL, DT_PAD)),
            const((D_MODEL, D_POOL)),
            const((SUBLANES, CONV_DIM)),
            const((1, CONV_DIM)),
            const((1, DT_PAD)),
            const((1, DT_PAD)),
            const((1, D_SSM)),
            const((CHUNK, DT_REP * CHUNK)),
            const((DT_PAD, N_HEADS * CHUNK)),
            const((DT_PAD, D_SSM)),
        ],
        out_specs=[
            pl.BlockSpec((tm, D_SSM), proj_row),
            pl.BlockSpec((tm, D_POOL), proj_row),
            pl.BlockSpec((tm, DT_PAD), proj_row),
            pl.BlockSpec((tm, CONV_DIM), scan_row),
            pl.BlockSpec((tm, D_SSM), scan_row),
        ],
        out_shape=[
            jax.ShapeDtypeStruct((t, D_SSM), F32),
            jax.ShapeDtypeStruct((t, D_POOL), F32),
            jax.ShapeDtypeStruct((t, DT_PAD), F32),
            jax.ShapeDtypeStruct((t, CONV_DIM), F32),
            jax.ShapeDtypeStruct((t, D_SSM), F32),
        ],
        scratch_shapes=[
            pltpu.VMEM((RING, tm, CONV_DIM), F32),
            pltpu.VMEM((RING, tm, DT_PAD), F32),
            pltpu.VMEM((GROUPS, D_STATE, GROUP_W), F32),
            pltpu.VMEM((tm, D_MODEL), BF16),
            pltpu.VMEM((tm, D_MODEL), BF16),
        ],
        compiler_params=pltpu.CompilerParams(
            dimension_semantics=("arbitrary", "arbitrary"), vmem_limit_bytes=VMEM_LIMIT),
        name="proj_ssd_fwd",
    )(x, x, g, wz, wxbc, wdt, wu, cw, cb, dtb, alog, dskip, tril, colsel, headsel)


def _ssd_bwd_kernel(xc_ref, dt_ref, yf_ref, z_ref, u_ref, uprev_ref, unext_ref, x_ref,
                    dtb_ref, alog_ref, triu_ref, colsel_ref, headsel_ref, normw_ref,
                    dmat_ref, poolw_ref, pscale_ref, wout_ref, gpost_ref,
                    o_ref, y_ref, extu_ref, mixin_ring, mix_ref, h_ref, *, tm, nt):
    j = pl.program_id(1)
    tile = nt - 1 - jnp.minimum(j, nt - 1)

    @pl.when(j == 0)
    def _():
        h_ref[...] = jnp.zeros_like(h_ref)
        mixin_ring[...] = jnp.zeros_like(mixin_ring)

    cur = lax.rem(j, 2)
    prv = 1 - cur

    def out_proj(lo):
        mix_ref[:, lo:lo + MXU_N] = _dot(mixin_ring[prv], wout_ref[:, lo:lo + MXU_N])

    pieces = [functools.partial(out_proj, lo) for lo in range(0, D_MODEL, MXU_N)]
    queue = iter(pieces)

    def fill(n=1):
        for _ in range(n):
            piece = next(queue, None)
            if piece is not None:
                piece()

    a_row = _dt_consts(alog_ref)
    dtb = dtb_ref[...]
    nchunk = tm // CHUNK
    for c in reversed(range(nchunk)):
        r0 = c * CHUNK
        y = _ssd_chunk(xc_ref[r0:r0 + CHUNK, :], dt_ref[r0:r0 + CHUNK, :], dtb, a_row,
                       triu_ref[...], colsel_ref[...], headsel_ref[...], h_ref, reverse=True,
                       fill=fill)
        y_ref[r0:r0 + CHUNK, :] = y + yf_ref[r0:r0 + CHUNK, :]
    fill(len(pieces))

    v = y_ref[...] * _silu(z_ref[...])
    for g in range(GROUPS):
        lo = g * GROUP_W
        mixin_ring[cur, :, lo:lo + GROUP_W] = (_rms(v[:, lo:lo + GROUP_W])
                                               * normw_ref[:, lo:lo + GROUP_W]).astype(BF16)

    extu_ref[0:HALO, :] = jnp.where(tile > 0, uprev_ref[...], 0.0)
    extu_ref[HALO:HALO + tm, :] = u_ref[...]
    extu_ref[HALO + tm:HALO + tm + HALO, :] = jnp.where(tile < nt - 1, unext_ref[...], 0.0)
    extu_ref[HALO + tm + HALO:, :] = jnp.zeros((POOL_K - CHUNK - 2 * HALO, D_POOL), F32)
    for c in range(nchunk):
        r0 = c * CHUNK
        first = jnp.logical_and(tile == 0, c == 0)
        final = jnp.logical_and(tile == nt - 1, c == nchunk - 1)
        variant = jnp.where(first, 0, jnp.where(final, 2, 1))
        for gi in range(len(POOL_WINDOWS)):
            lo = gi * POOL_GROUP_DIM
            band = dmat_ref[variant * len(POOL_WINDOWS) + gi]
            src = extu_ref[r0:r0 + POOL_K, lo:lo + POOL_GROUP_DIM].astype(BF16)
            d = _dot(band, src).astype(BF16)
            pooled = _dot(d, poolw_ref[gi]) * pscale_ref[:, lo:lo + POOL_GROUP_DIM]
            mixin_ring[cur, r0:r0 + CHUNK,
                       D_SSM + lo:D_SSM + lo + POOL_GROUP_DIM] = pooled.astype(BF16)

    o_ref[...] = x_ref[...] + _rms(mix_ref[...]) * gpost_ref[...]


def _ssd_bwd(xc, dt, yf, z, u, x, dtb, alog, triu, colsel, headsel, normw, dmat, poolw, pscale,
             wout, gpost, *, nb, seq, tm):
    nt = seq // tm
    hb = tm // HALO
    nblk8 = nb * seq // HALO
    tile = lambda b, j: b * nt + nt - 1 - jnp.minimum(j, nt - 1)
    row = lambda b, j: (tile(b, j), 0)
    prev = lambda b, j: (jnp.maximum(tile(b, j) * hb - 1, 0), 0)
    nxt = lambda b, j: (jnp.minimum((tile(b, j) + 1) * hb, nblk8 - 1), 0)
    out_row = lambda b, j: (b * nt + jnp.minimum(nt - j, nt - 1), 0)
    const2 = functools.partial(pl.BlockSpec, index_map=lambda b, j: (0, 0),
                               pipeline_mode=pl.Buffered(1))
    const3 = functools.partial(pl.BlockSpec, index_map=lambda b, j: (0, 0, 0),
                               pipeline_mode=pl.Buffered(1))
    t = nb * seq
    nvar = 3 * len(POOL_WINDOWS)
    return pl.pallas_call(
        functools.partial(_ssd_bwd_kernel, tm=tm, nt=nt),
        grid=(nb, nt + 1),
        in_specs=[
            pl.BlockSpec((tm, CONV_DIM), row),
            pl.BlockSpec((tm, DT_PAD), row),
            pl.BlockSpec((tm, D_SSM), row),
            pl.BlockSpec((tm, D_SSM), row),
            pl.BlockSpec((tm, D_POOL), row),
            pl.BlockSpec((HALO, D_POOL), prev),
            pl.BlockSpec((HALO, D_POOL), nxt),
            pl.BlockSpec((tm, D_MODEL), out_row),
            const2((1, DT_PAD)),
            const2((1, DT_PAD)),
            const2((CHUNK, DT_REP * CHUNK)),
            const2((DT_PAD, N_HEADS * CHUNK)),
            const2((DT_PAD, D_SSM)),
            const2((1, D_SSM)),
            const3((nvar, CHUNK, POOL_K)),
            const3((len(POOL_WINDOWS), POOL_GROUP_DIM, POOL_GROUP_DIM)),
            const2((1, D_POOL)),
            const2((D_SSM + D_POOL, D_MODEL)),
            const2((1, D_MODEL)),
        ],
        out_specs=pl.BlockSpec((tm, D_MODEL), out_row),
        out_shape=jax.ShapeDtypeStruct((t, D_MODEL), F32),
        scratch_shapes=[
            pltpu.VMEM((tm, D_SSM), F32),
            pltpu.VMEM((tm + CHUNK, D_POOL), F32),
            pltpu.VMEM((2, tm, D_SSM + D_POOL), BF16),
            pltpu.VMEM((tm, D_MODEL), F32),
            pltpu.VMEM((GROUPS, D_STATE, GROUP_W), F32),
        ],
        compiler_params=pltpu.CompilerParams(
            dimension_semantics=("arbitrary", "arbitrary"), vmem_limit_bytes=VMEM_LIMIT),
        name="ssd_bwd_mix",
    )(xc, dt, yf, z, u, u, u, x, dtb, alog, triu, colsel, headsel, normw, dmat, poolw, pscale,
      wout, gpost)


def _mlp_kernel(x_ref, gpre_ref, w1_ref, w2_ref, gpost_ref, o_ref, h_ref, acc_ref):
    f = pl.program_id(1)

    @pl.when(f == 0)
    def _():
        h_ref[...] = (_rms(x_ref[...]) * gpre_ref[...]).astype(BF16)
        acc_ref[...] = jnp.zeros_like(acc_ref)

    a = jnp.maximum(_dot(h_ref[...], w1_ref[...]), 0.0)
    acc_ref[...] += _dot((a * a).astype(BF16), w2_ref[...])

    @pl.when(f == pl.num_programs(1) - 1)
    def _():
        o_ref[...] = x_ref[...] + _rms(acc_ref[...]) * gpost_ref[...]


def _mlp(x, gpre, w1, w2, gpost, *, tm, tf):
    t = x.shape[0]
    row = lambda i, f: (i, 0)
    fixed = lambda i, f: (0, 0)
    const = functools.partial(pl.BlockSpec, index_map=fixed, pipeline_mode=pl.Buffered(1))
    return pl.pallas_call(
        _mlp_kernel,
        grid=(t // tm, D_FF // tf),
        in_specs=[
            pl.BlockSpec((tm, D_MODEL), row),
            const((1, D_MODEL)),
            pl.BlockSpec((D_MODEL, tf), lambda i, f: (0, f)),
            pl.BlockSpec((tf, D_MODEL), lambda i, f: (f, 0)),
            const((1, D_MODEL)),
        ],
        out_specs=pl.BlockSpec((tm, D_MODEL), row),
        out_shape=jax.ShapeDtypeStruct((t, D_MODEL), F32),
        scratch_shapes=[
            pltpu.VMEM((tm, D_MODEL), BF16),
            pltpu.VMEM((tm, D_MODEL), F32),
        ],
        compiler_params=pltpu.CompilerParams(
            dimension_semantics=("arbitrary", "arbitrary"), vmem_limit_bytes=VMEM_LIMIT),
        name="mlp",
    )(x, gpre, w1, w2, gpost)


def _selectors():
    q = np.arange(CHUNK)[:, None]
    s = np.arange(CHUNK)[None, :]
    tril = np.tile((s <= q).astype(np.float32), (1, DT_REP))
    triu = np.tile((s >= q).astype(np.float32), (1, DT_REP))
    colsel = np.zeros((2, DT_PAD, N_HEADS * CHUNK), np.float32)
    headsel = np.zeros((2, DT_PAD, D_SSM), np.float32)
    for d in range(2):
        for h in range(N_HEADS):
            for rep in range(DT_REP):
                src = rep * DT_COLS + d * N_HEADS + h
                colsel[d, src, h * CHUNK:(h + 1) * CHUNK] = 1.0
                headsel[d, src, h * HEAD_DIM:(h + 1) * HEAD_DIM] = 1.0
    as_bf16 = lambda a: jnp.asarray(a, dtype=BF16)
    return as_bf16(tril), as_bf16(triu), as_bf16(colsel), as_bf16(headsel)


def _pool_bands(seq):
    out = np.zeros((3, len(POOL_WINDOWS), CHUNK, POOL_K), np.float32)
    for v, start in enumerate((0, CHUNK, seq - CHUNK)):
        t = start + np.arange(CHUNK)[:, None]
        src = start + np.arange(POOL_K)[None, :] - HALO
        for gi, w in enumerate(POOL_WINDOWS):
            hi = np.minimum(t + w // 2, seq)
            lo = np.maximum(t - w // 2, 0)
            inside = (src >= lo) & (src < hi)
            out[v, gi] = inside / (hi - lo).astype(np.float32) - (src == t)
    return jnp.asarray(out.reshape(3 * len(POOL_WINDOWS), CHUNK, POOL_K), dtype=BF16)


def _layer_params(i, g_mix_pre, g_mix_post, g_mlp_pre, g_mlp_post, w_in, conv_w, conv_b,
                  dt_bias, a_log, d_skip, ssm_norm_w, pool_w, pool_scale, w_out,
                  w_mlp_in, w_mlp_out):
    row = lambda v: v.reshape(1, -1).astype(F32)
    rep_lanes = lambda v: jnp.pad(jnp.tile(v.astype(F32).reshape(-1, DT_COLS), (1, DT_REP)),
                                  ((0, 0), (0, DT_PAD - DT_REP * DT_COLS)))
    o_dt = D_SSM + CONV_DIM
    w = w_in[i]
    return dict(
        g_mix_pre=row(g_mix_pre[i]), g_mix_post=row(g_mix_post[i]),
        g_mlp_pre=row(g_mlp_pre[i]), g_mlp_post=row(g_mlp_post[i]),
        wz=w[:, :D_SSM].astype(BF16),
        wxbc=w[:, D_SSM:o_dt].astype(BF16),
        wdt=rep_lanes(w[:, o_dt:o_dt + DT_COLS]).astype(BF16),
        wu=w[:, o_dt + DT_COLS:].astype(BF16),
        cw=jnp.pad(conv_w[i].astype(F32), ((0, SUBLANES - CONV_K), (0, 0))),
        cb=row(conv_b[i]),
        dtb=rep_lanes(dt_bias[i]), alog=rep_lanes(a_log[i]),
        dskip=row(jnp.repeat(d_skip[i], HEAD_DIM)),
        normw=row(ssm_norm_w[i]),
        poolw=pool_w[i].astype(BF16), pscale=row(pool_scale[i]),
        wout=w_out[i].astype(BF16),
        w1=w_mlp_in[i].astype(BF16), w2=w_mlp_out[i].astype(BF16),
    )


def _trunk(x, layers, consts, *, tm_ssd, tm_mlp, tf):
    nb, seq, _ = x.shape
    tril, triu, colsel, headsel = consts
    dmat = _pool_bands(seq)
    x = x.reshape(nb * seq, D_MODEL)
    for p in layers:
        z, u, dt, xc, yf = _proj_fwd(x, p["g_mix_pre"], p["wz"], p["wxbc"], p["wdt"], p["wu"],
                                     p["cw"], p["cb"], p["dtb"], p["alog"], p["dskip"],
                                     tril, colsel[0], headsel[0], nb=nb, seq=seq, tm=tm_ssd)
        x = _ssd_bwd(xc, dt, yf, z, u, x, p["dtb"], p["alog"], triu, colsel[1], headsel[1],
                     p["normw"], dmat, p["poolw"], p["pscale"], p["wout"], p["g_mix_post"],
                     nb=nb, seq=seq, tm=tm_ssd)
        x = _mlp(x, p["g_mlp_pre"], p["w1"], p["w2"], p["g_mlp_post"], tm=tm_mlp, tf=tf)
    return x.reshape(nb, seq, D_MODEL)


def kernel(x_prompt, x_sample, g_mix_pre, g_mix_post, g_mlp_pre, g_mlp_post, w_in, conv_w, conv_b, dt_bias, a_log, d_skip, ssm_norm_w, pool_w, pool_scale, w_out, w_mlp_in, w_mlp_out):
    depth = w_in.shape[0]
    layers = [_layer_params(i, g_mix_pre, g_mix_post, g_mlp_pre, g_mlp_post, w_in, conv_w,
                            conv_b, dt_bias, a_log, d_skip, ssm_norm_w, pool_w, pool_scale,
                            w_out, w_mlp_in, w_mlp_out) for i in range(depth)]
    consts = _selectors()
    run = functools.partial(_trunk, layers=layers, consts=consts,
                            tm_ssd=2 * CHUNK, tm_mlp=512, tf=1024)
    return (run(x_prompt), run(x_sample))
```

```python
import functools

import numpy as np
import jax
import jax.numpy as jnp
from jax import lax
from jax.experimental import pallas as pl
from jax.experimental.pallas import tpu as pltpu

D_MODEL = 2048
D_SSM = 1024
HEAD_DIM = 64
N_HEADS = 16
GROUPS = 2
HEADS_PER_GROUP = N_HEADS // GROUPS
GROUP_W = HEADS_PER_GROUP * HEAD_DIM
D_STATE = 128
CONV_K = 5
CHUNK = 128
D_POOL = 1024
POOL_WINDOWS = (2, 4, 8, 16)
POOL_GROUP_DIM = D_POOL // len(POOL_WINDOWS)
CONV_DIM = D_SSM + 2 * GROUPS * D_STATE
D_FF = 4 * D_MODEL
NORM_EPS = 1e-6

LANES = 128
SUBLANES = 8
HALO = SUBLANES
DT_PAD = LANES
DT_COLS = 2 * N_HEADS
DT_REP = 3
HEADS_PER_DOT = 4
POOL_K = 2 * CHUNK
MXU_N = 256
MLP_ROWS = 128
FWD_LAG = 2
FWD_FILL_CONV = 1
FWD_FILL_PLAN = (1, 1, 1, 1, 1, 1)
BWD_FILL_PLAN = (1, 0, 1, 0, 1, 0)
RING = 4
NEG_BIG = -1e30
VMEM_LIMIT = 56 * 1024 * 1024

F32 = jnp.float32
BF16 = jnp.bfloat16


def _dot(a, b):
    return jnp.dot(a, b, preferred_element_type=F32)


def _dot_nt(a, b):
    return lax.dot_general(a, b, (((1,), (1,)), ((), ())), preferred_element_type=F32)


def _split3(v):
    hi = v.astype(BF16)
    r1 = v - hi.astype(F32)
    lo = r1.astype(BF16)
    lo2 = (r1 - lo.astype(F32)).astype(BF16)
    return hi, lo, lo2


def _lane_split(v):
    hi, lo, lo2 = _split3(v)
    lane = lax.broadcasted_iota(jnp.int32, v.shape, 1)
    return jnp.where(lane < DT_COLS, hi, jnp.where(lane < 2 * DT_COLS, lo, lo2))


def _exact_rdot(v, sel):
    return _dot(_lane_split(v), sel)


def _exact_cumsum(sel3, v):
    return _dot(sel3, jnp.concatenate(_split3(v), axis=0))


def _silu(v):
    return v * (1.0 / (1.0 + jnp.exp(-v)))


def _softplus(v):
    return jnp.maximum(v, 0.0) + jnp.log1p(jnp.exp(-jnp.abs(v)))


def _rms(v):
    return v * lax.rsqrt(jnp.mean(v * v, axis=-1, keepdims=True) + NORM_EPS)


def _ssd_chunk(xc, dtr, dtb, a_row, cum_sel, colsel, headsel, h_ref, *, reverse, fresh=None,
               fill=lambda: None):
    lane_off = N_HEADS if reverse else 0
    edge = 0 if reverse else CHUNK - 1
    xs = xc[:, :D_SSM]
    xs_b = xs.astype(BF16)

    dt = _softplus(dtr + dtb)
    acum = _exact_cumsum(cum_sel, dt * a_row)
    fill()
    g_t = (acum - jnp.log(dt)).T
    a_col = _exact_rdot(acum, colsel)
    a_edge = acum[edge:edge + 1, :]
    w_exp = _exact_rdot(jnp.exp(a_edge - acum) * dt, headsel)
    e_exp = _exact_rdot(jnp.exp(acum), headsel)
    fill()
    xw_b = (w_exp * xs).astype(BF16)

    q_idx = lax.broadcasted_iota(jnp.int32, (CHUNK, CHUNK), 0)
    s_idx = lax.broadcasted_iota(jnp.int32, (CHUNK, CHUNK), 1)
    allowed = (s_idx >= q_idx) if reverse else (s_idx <= q_idx)
    quad_w = HEADS_PER_DOT * HEAD_DIM
    head_of_lane = lax.broadcasted_iota(jnp.int32, (CHUNK, quad_w), 1) // HEAD_DIM

    pieces = []
    for g in range(GROUPS):
        b_g = xc[:, D_SSM + g * D_STATE:D_SSM + (g + 1) * D_STATE]
        c_g = xc[:, D_SSM + GROUPS * D_STATE + g * D_STATE:
                 D_SSM + GROUPS * D_STATE + (g + 1) * D_STATE].astype(BF16)
        scores = _dot_nt(c_g, b_g.astype(BF16))
        h_prev = h_ref[g]
        if fresh is not None:
            h_prev = jnp.where(fresh, 0.0, h_prev)
        y_off = _dot(c_g, h_prev.astype(BF16))
        for qd in range(HEADS_PER_GROUP // HEADS_PER_DOT):
            h0 = g * HEADS_PER_GROUP + HEADS_PER_DOT * qd
            lo = h0 * HEAD_DIM
            wmats = []
            for hh in range(h0, h0 + HEADS_PER_DOT):
                seg = (a_col[:, hh * CHUNK:(hh + 1) * CHUNK]
                       - g_t[lane_off + hh:lane_off + hh + 1, :])
                wmats.append((scores * jnp.exp(jnp.where(allowed, seg, NEG_BIG))).astype(BF16))
            x_quad = xs_b[:, lo:lo + quad_w]
            x_diag = jnp.concatenate(
                [jnp.where(head_of_lane == k, x_quad, jnp.zeros_like(x_quad))
                 for k in range(HEADS_PER_DOT)], axis=0)
            y_diag = _dot(jnp.concatenate(wmats, axis=1), x_diag)
            fill()
            pieces.append(y_diag + y_off[:, qd * quad_w:(qd + 1) * quad_w]
                          * e_exp[:, lo:lo + quad_w])
        states = _dot(b_g.T.astype(BF16), xw_b[:, g * GROUP_W:(g + 1) * GROUP_W])
        decay = e_exp[edge:edge + 1, g * GROUP_W:(g + 1) * GROUP_W]
        h_ref[g] = decay * h_prev + states
    return jnp.concatenate(pieces, axis=-1)


def _dt_consts(alog_ref):
    lane = lax.broadcasted_iota(jnp.int32, (1, DT_PAD), 1)
    return jnp.where(lane < DT_REP * DT_COLS, -jnp.exp(alog_ref[...]), 0.0)


def _proj_fwd_kernel(x0_ref, xn_ref, g_ref, wz_ref, wxbc_ref, wdt_ref, wu_ref, cw_ref, cb_ref,
                     dtb_ref, alog_ref, dskip_ref, tril_ref, colsel_ref, headsel_ref,
                     z_ref, u_ref, dt_ref, xc_ref, yf_ref,
                     xbc_ring, dt_ring, h_ref, hn_ref, hn_next_ref, *, tm, nt):
    i = pl.program_id(1)

    @pl.when(i == 0)
    def _():
        xbc_ring[...] = jnp.zeros_like(xbc_ring)
        dt_ring[...] = jnp.zeros_like(dt_ring)
        h_ref[...] = jnp.zeros_like(h_ref)
        hn_next_ref[...] = (_rms(x0_ref[...]) * g_ref[...]).astype(BF16)

    t = i - FWD_LAG
    cur = lax.rem(i + RING - FWD_LAG, RING)
    prv = lax.rem(i + RING - FWD_LAG - 1, RING)
    nxt = lax.rem(i + RING - FWD_LAG + 1, RING)
    head = jnp.where(t > 0, xbc_ring[prv, tm - HALO:tm, :], 0.0)
    tail = jnp.where(t < nt - 1, xbc_ring[nxt, 0:HALO, :], 0.0)
    nchunk = tm // CHUNK
    mid = CONV_K // 2
    rows = CHUNK + 2 * HALO
    slot = lax.rem(i, RING)
    a_row = _dt_consts(alog_ref)
    dtb = dtb_ref[...]
    dts = dt_ring[cur]
    hn_ref[...] = hn_next_ref[...]

    def conv(c):
        r0 = c * CHUNK
        blk = jnp.concatenate([
            head if c == 0 else xbc_ring[cur, r0 - HALO:r0, :],
            xbc_ring[cur, r0:r0 + CHUNK, :],
            tail if c == nchunk - 1 else xbc_ring[cur, r0 + CHUNK:r0 + CHUNK + HALO, :],
        ], axis=0)
        acc = cb_ref[...] + cw_ref[mid:mid + 1, :] * blk[HALO:HALO + CHUNK, :]
        for k in range(CONV_K):
            if k != mid:
                rolled = pltpu.roll(blk, rows - (HALO - mid + k), axis=0)
                acc = acc + cw_ref[k:k + 1, :] * rolled[0:CHUNK, :]
        xc_ref[r0:r0 + CHUNK, :] = _silu(acc)

    def scan(c):
        r0 = c * CHUNK
        xc = xc_ref[r0:r0 + CHUNK, :]
        y = _ssd_chunk(xc, dts[r0:r0 + CHUNK, :], dtb, a_row, tril_ref[...], colsel_ref[...],
                       headsel_ref[...], h_ref, reverse=False,
                       fresh=(t <= 0) if c == 0 else None, fill=fill)
        yf_ref[r0:r0 + CHUNK, :] = y + dskip_ref[...] * xc[:, :D_SSM]

    def proj_z(lo):
        z_ref[:, lo:lo + MXU_N] = _dot(hn_ref[...], wz_ref[:, lo:lo + MXU_N])

    def proj_u(lo):
        u_ref[:, lo:lo + MXU_N] = _dot(hn_ref[...], wu_ref[:, lo:lo + MXU_N])

    def proj_dt():
        dtr = _dot(hn_ref[...], wdt_ref[...])
        dt_ref[...] = dtr
        dt_ring[slot] = dtr

    def proj_xbc(lo):
        xbc_ring[slot, :, lo:lo + MXU_N] = _dot(hn_ref[...], wxbc_ref[:, lo:lo + MXU_N])

    pieces = ([functools.partial(proj_z, lo) for lo in range(0, D_SSM, MXU_N)]
              + [functools.partial(proj_u, lo) for lo in range(0, D_POOL, MXU_N)]
              + [proj_dt]
              + [functools.partial(proj_xbc, lo) for lo in range(0, CONV_DIM, MXU_N)])
    queue = iter(pieces)

    def emit(n):
        for _ in range(n):
            piece = next(queue, None)
            if piece is not None:
                piece()

    plan = iter(FWD_FILL_PLAN * nchunk)

    def fill():
        emit(next(plan, 0))

    for c in range(nchunk):
        emit(FWD_FILL_CONV)
        conv(c)
    for c in range(nchunk):
        scan(c)
    emit(len(pieces))
    hn_next_ref[...] = (_rms(xn_ref[...]) * g_ref[...]).astype(BF16)


def _proj_fwd(x, g, wz, wxbc, wdt, wu, cw, cb, dtb, alog, dskip, tril, colsel, headsel,
              *, nb, seq, tm):
    nt = seq // tm
    proj_row = lambda b, i: (b * nt + jnp.minimum(i, nt - 1), 0)
    norm_row = lambda b, i: (b * nt + jnp.minimum(i + 1, nt - 1), 0)
    scan_row = lambda b, i: (b * nt + jnp.clip(i - FWD_LAG, 0, nt - 1), 0)
    const = functools.partial(pl.BlockSpec, index_map=lambda b, i: (0, 0),
                              pipeline_mode=pl.Buffered(1))
    t = nb * seq
    return pl.pallas_call(
        functools.partial(_proj_fwd_kernel, tm=tm, nt=nt),
        grid=(nb, nt + FWD_LAG),
        in_specs=[
            pl.BlockSpec((tm, D_MODEL), lambda b, i: (b * nt, 0)),
            pl.BlockSpec((tm, D_MODEL), norm_row),
            const((1, D_MODEL)),
            const((D_MODEL, D_SSM)),
            const((D_MODEL, CONV_DIM)),
            const((D_MODEL, DT_PAD)),
            const((D_MODEL, D_POOL)),
            const((SUBLANES, CONV_DIM)),
            const((1, CONV_DIM)),
            const((1, DT_PAD)),
            const((1, DT_PAD)),
            const((1, D_SSM)),
            const((CHUNK, DT_REP * CHUNK)),
            const((DT_PAD, N_HEADS * CHUNK)),
            const((DT_PAD, D_SSM)),
        ],
        out_specs=[
            pl.BlockSpec((tm, D_SSM), proj_row),
            pl.BlockSpec((tm, D_POOL), proj_row),
            pl.BlockSpec((tm, DT_PAD), proj_row),
            pl.BlockSpec((tm, CONV_DIM), scan_row),
            pl.BlockSpec((tm, D_SSM), scan_row),
        ],
        out_shape=[
            jax.ShapeDtypeStruct((t, D_SSM), F32),
            jax.ShapeDtypeStruct((t, D_POOL), F32),
            jax.ShapeDtypeStruct((t, DT_PAD), F32),
            jax.ShapeDtypeStruct((t, CONV_DIM), F32),
            jax.ShapeDtypeStruct((t, D_SSM), F32),
        ],
        scratch_shapes=[
            pltpu.VMEM((RING, tm, CONV_DIM), F32),
            pltpu.VMEM((RING, tm, DT_PAD), F32),
            pltpu.VMEM((GROUPS, D_STATE, GROUP_W), F32),
            pltpu.VMEM((tm, D_MODEL), BF16),
            pltpu.VMEM((tm, D_MODEL), BF16),
        ],
        compiler_params=pltpu.CompilerParams(
            dimension_semantics=("arbitrary", "arbitrary"), vmem_limit_bytes=VMEM_LIMIT),
        name="proj_ssd_fwd",
    )(x, x, g, wz, wxbc, wdt, wu, cw, cb, dtb, alog, dskip, tril, colsel, headsel)


def _ssd_bwd_kernel(xc_ref, dt_ref, yf_ref, z_ref, u_ref, uprev_ref, unext_ref, x_ref,
                    dtb_ref, alog_ref, triu_ref, colsel_ref, headsel_ref, normw_ref,
                    dmat_ref, poolw_ref, pscale_ref, wout_ref, gpost_ref,
                    o_ref, y_ref, extu_ref, mixin_ring, mix_ref, h_ref, *, tm, nt):
    j = pl.program_id(1)
    tile = nt - 1 - jnp.minimum(j, nt - 1)

    @pl.when(j == 0)
    def _():
        h_ref[...] = jnp.zeros_like(h_ref)
        mixin_ring[...] = jnp.zeros_like(mixin_ring)

    cur = lax.rem(j, 2)
    prv = 1 - cur

    def out_proj(lo):
        mix_ref[:, lo:lo + MXU_N] = _dot(mixin_ring[prv], wout_ref[:, lo:lo + MXU_N])

    pieces = [functools.partial(out_proj, lo) for lo in range(0, D_MODEL, MXU_N)]
    queue = iter(pieces)

    def emit(n):
        for _ in range(n):
            piece = next(queue, None)
            if piece is not None:
                piece()

    nchunk = tm // CHUNK
    plan = iter(BWD_FILL_PLAN * nchunk)

    def fill():
        emit(next(plan, 0))

    a_row = _dt_consts(alog_ref)
    dtb = dtb_ref[...]
    for c in reversed(range(nchunk)):
        r0 = c * CHUNK
        y = _ssd_chunk(xc_ref[r0:r0 + CHUNK, :], dt_ref[r0:r0 + CHUNK, :], dtb, a_row,
                       triu_ref[...], colsel_ref[...], headsel_ref[...], h_ref, reverse=True,
                       fill=fill)
        y_ref[r0:r0 + CHUNK, :] = y + yf_ref[r0:r0 + CHUNK, :]
    emit(len(pieces))

    v = y_ref[...] * _silu(z_ref[...])
    for g in range(GROUPS):
        lo = g * GROUP_W
        mixin_ring[cur, :, lo:lo + GROUP_W] = (_rms(v[:, lo:lo + GROUP_W])
                                               * normw_ref[:, lo:lo + GROUP_W]).astype(BF16)

    extu_ref[0:HALO, :] = jnp.where(tile > 0, uprev_ref[...], 0.0)
    extu_ref[HALO:HALO + tm, :] = u_ref[...]
    extu_ref[HALO + tm:HALO + tm + HALO, :] = jnp.where(tile < nt - 1, unext_ref[...], 0.0)
    extu_ref[HALO + tm + HALO:, :] = jnp.zeros((POOL_K - CHUNK - 2 * HALO, D_POOL), F32)
    for c in range(nchunk):
        r0 = c * CHUNK
        first = jnp.logical_and(tile == 0, c == 0)
        final = jnp.logical_and(tile == nt - 1, c == nchunk - 1)
        variant = jnp.where(first, 0, jnp.where(final, 2, 1))
        for gi in range(len(POOL_WINDOWS)):
            lo = gi * POOL_GROUP_DIM
            band = dmat_ref[variant * len(POOL_WINDOWS) + gi]
            src = extu_ref[r0:r0 + POOL_K, lo:lo + POOL_GROUP_DIM].astype(BF16)
            d = _dot(band, src).astype(BF16)
            pooled = _dot(d, poolw_ref[gi]) * pscale_ref[:, lo:lo + POOL_GROUP_DIM]
            mixin_ring[cur, r0:r0 + CHUNK,
                       D_SSM + lo:D_SSM + lo + POOL_GROUP_DIM] = pooled.astype(BF16)

    o_ref[...] = x_ref[...] + _rms(mix_ref[...]) * gpost_ref[...]


def _ssd_bwd(xc, dt, yf, z, u, x, dtb, alog, triu, colsel, headsel, normw, dmat, poolw, pscale,
             wout, gpost, *, nb, seq, tm):
    nt = seq // tm
    hb = tm // HALO
    nblk8 = nb * seq // HALO
    tile = lambda b, j: b * nt + nt - 1 - jnp.minimum(j, nt - 1)
    row = lambda b, j: (tile(b, j), 0)
    prev = lambda b, j: (jnp.maximum(tile(b, j) * hb - 1, 0), 0)
    nxt = lambda b, j: (jnp.minimum((tile(b, j) + 1) * hb, nblk8 - 1), 0)
    out_row = lambda b, j: (b * nt + jnp.minimum(nt - j, nt - 1), 0)
    const2 = functools.partial(pl.BlockSpec, index_map=lambda b, j: (0, 0),
                               pipeline_mode=pl.Buffered(1))
    const3 = functools.partial(pl.BlockSpec, index_map=lambda b, j: (0, 0, 0),
                               pipeline_mode=pl.Buffered(1))
    t = nb * seq
    nvar = 3 * len(POOL_WINDOWS)
    return pl.pallas_call(
        functools.partial(_ssd_bwd_kernel, tm=tm, nt=nt),
        grid=(nb, nt + 1),
        in_specs=[
            pl.BlockSpec((tm, CONV_DIM), row),
            pl.BlockSpec((tm, DT_PAD), row),
            pl.BlockSpec((tm, D_SSM), row),
            pl.BlockSpec((tm, D_SSM), row),
            pl.BlockSpec((tm, D_POOL), row),
            pl.BlockSpec((HALO, D_POOL), prev),
            pl.BlockSpec((HALO, D_POOL), nxt),
            pl.BlockSpec((tm, D_MODEL), out_row),
            const2((1, DT_PAD)),
            const2((1, DT_PAD)),
            const2((CHUNK, DT_REP * CHUNK)),
            const2((DT_PAD, N_HEADS * CHUNK)),
            const2((DT_PAD, D_SSM)),
            const2((1, D_SSM)),
            const3((nvar, CHUNK, POOL_K)),
            const3((len(POOL_WINDOWS), POOL_GROUP_DIM, POOL_GROUP_DIM)),
            const2((1, D_POOL)),
            const2((D_SSM + D_POOL, D_MODEL)),
            const2((1, D_MODEL)),
        ],
        out_specs=pl.BlockSpec((tm, D_MODEL), out_row),
        out_shape=jax.ShapeDtypeStruct((t, D_MODEL), F32),
        scratch_shapes=[
            pltpu.VMEM((tm, D_SSM), F32),
            pltpu.VMEM((tm + CHUNK, D_POOL), F32),
            pltpu.VMEM((2, tm, D_SSM + D_POOL), BF16),
            pltpu.VMEM((tm, D_MODEL), F32),
            pltpu.VMEM((GROUPS, D_STATE, GROUP_W), F32),
        ],
        compiler_params=pltpu.CompilerParams(
            dimension_semantics=("arbitrary", "arbitrary"), vmem_limit_bytes=VMEM_LIMIT),
        name="ssd_bwd_mix",
    )(xc, dt, yf, z, u, u, u, x, dtb, alog, triu, colsel, headsel, normw, dmat, poolw, pscale,
      wout, gpost)


def _mlp_kernel(x_ref, gpre_ref, w1_ref, w2_ref, gpost_ref, o_ref, h_ref, acc_ref, act_ref):
    f = pl.program_id(1)
    last = pl.num_programs(1) - 1
    blocks = [slice(r, r + MLP_ROWS) for r in range(0, x_ref.shape[0], MLP_ROWS)]

    def sq_relu(a):
        a = jnp.maximum(a, 0.0)
        return (a * a).astype(BF16)

    @pl.when(f == 0)
    def _():
        for rs in blocks:
            h_ref[rs, :] = (_rms(x_ref[rs, :]) * gpre_ref[...]).astype(BF16)
            act_ref[rs, :] = sq_relu(_dot(h_ref[rs, :], w1_ref[...]))
        acc_ref[...] = _dot(act_ref[...], w2_ref[...])

    @pl.when(jnp.logical_and(f > 0, f < last))
    def _():
        acc_ref[...] += _dot(sq_relu(_dot(h_ref[...], w1_ref[...])), w2_ref[...])

    @pl.when(f == last)
    def _():
        act_ref[...] = sq_relu(_dot(h_ref[...], w1_ref[...]))
        for rs in blocks:
            m = acc_ref[rs, :] + _dot(act_ref[rs, :], w2_ref[...])
            o_ref[rs, :] = x_ref[rs, :] + _rms(m) * gpost_ref[...]


def _mlp(x, gpre, w1, w2, gpost, *, tm, tf):
    t = x.shape[0]
    row = lambda i, f: (i, 0)
    fixed = lambda i, f: (0, 0)
    const = functools.partial(pl.BlockSpec, index_map=fixed, pipeline_mode=pl.Buffered(1))
    return pl.pallas_call(
        _mlp_kernel,
        grid=(t // tm, D_FF // tf),
        in_specs=[
            pl.BlockSpec((tm, D_MODEL), row),
            const((1, D_MODEL)),
            pl.BlockSpec((D_MODEL, tf), lambda i, f: (0, f)),
            pl.BlockSpec((tf, D_MODEL), lambda i, f: (f, 0)),
            const((1, D_MODEL)),
        ],
        out_specs=pl.BlockSpec((tm, D_MODEL), row),
        out_shape=jax.ShapeDtypeStruct((t, D_MODEL), F32),
        scratch_shapes=[
            pltpu.VMEM((tm, D_MODEL), BF16),
            pltpu.VMEM((tm, D_MODEL), F32),
            pltpu.VMEM((tm, tf), BF16),
        ],
        compiler_params=pltpu.CompilerParams(
            dimension_semantics=("arbitrary", "arbitrary"), vmem_limit_bytes=VMEM_LIMIT),
        name="mlp",
    )(x, gpre, w1, w2, gpost)


def _selectors():
    q = np.arange(CHUNK)[:, None]
    s = np.arange(CHUNK)[None, :]
    tril = np.tile((s <= q).astype(np.float32), (1, DT_REP))
    triu = np.tile((s >= q).astype(np.float32), (1, DT_REP))
    colsel = np.zeros((2, DT_PAD, N_HEADS * CHUNK), np.float32)
    headsel = np.zeros((2, DT_PAD, D_SSM), np.float32)
    for d in range(2):
        for h in range(N_HEADS):
            for rep in range(DT_REP):
                src = rep * DT_COLS + d * N_HEADS + h
                colsel[d, src, h * CHUNK:(h + 1) * CHUNK] = 1.0
                headsel[d, src, h * HEAD_DIM:(h + 1) * HEAD_DIM] = 1.0
    as_bf16 = lambda a: jnp.asarray(a, dtype=BF16)
    return as_bf16(tril), as_bf16(triu), as_bf16(colsel), as_bf16(headsel)


def _pool_bands(seq):
    out = np.zeros((3, len(POOL_WINDOWS), CHUNK, POOL_K), np.float32)
    for v, start in enumerate((0, CHUNK, seq - CHUNK)):
        t = start + np.arange(CHUNK)[:, None]
        src = start + np.arange(POOL_K)[None, :] - HALO
        for gi, w in enumerate(POOL_WINDOWS):
            hi = np.minimum(t + w // 2, seq)
            lo = np.maximum(t - w // 2, 0)
            inside = (src >= lo) & (src < hi)
            out[v, gi] = inside / (hi - lo).astype(np.float32) - (src == t)
    return jnp.asarray(out.reshape(3 * len(POOL_WINDOWS), CHUNK, POOL_K), dtype=BF16)


def _layer_params(i, g_mix_pre, g_mix_post, g_mlp_pre, g_mlp_post, w_in, conv_w, conv_b,
                  dt_bias, a_log, d_skip, ssm_norm_w, pool_w, pool_scale, w_out,
                  w_mlp_in, w_mlp_out):
    row = lambda v: v.reshape(1, -1).astype(F32)
    rep_lanes = lambda v: jnp.pad(jnp.tile(v.astype(F32).reshape(-1, DT_COLS), (1, DT_REP)),
                                  ((0, 0), (0, DT_PAD - DT_REP * DT_COLS)))
    o_dt = D_SSM + CONV_DIM
    w = w_in[i]
    return dict(
        g_mix_pre=row(g_mix_pre[i]), g_mix_post=row(g_mix_post[i]),
        g_mlp_pre=row(g_mlp_pre[i]), g_mlp_post=row(g_mlp_post[i]),
        wz=w[:, :D_SSM].astype(BF16),
        wxbc=w[:, D_SSM:o_dt].astype(BF16),
        wdt=rep_lanes(w[:, o_dt:o_dt + DT_COLS]).astype(BF16),
        wu=w[:, o_dt + DT_COLS:].astype(BF16),
        cw=jnp.pad(conv_w[i].astype(F32), ((0, SUBLANES - CONV_K), (0, 0))),
        cb=row(conv_b[i]),
        dtb=rep_lanes(dt_bias[i]), alog=rep_lanes(a_log[i]),
        dskip=row(jnp.repeat(d_skip[i], HEAD_DIM)),
        normw=row(ssm_norm_w[i]),
        poolw=pool_w[i].astype(BF16), pscale=row(pool_scale[i]),
        wout=w_out[i].astype(BF16),
        w1=w_mlp_in[i].astype(BF16), w2=w_mlp_out[i].astype(BF16),
    )


def _trunk(x, layers, consts, *, tm_ssd, tm_mlp, tf):
    nb, seq, _ = x.shape
    tril, triu, colsel, headsel = consts
    dmat = _pool_bands(seq)
    x = x.reshape(nb * seq, D_MODEL)
    for p in layers:
        z, u, dt, xc, yf = _proj_fwd(x, p["g_mix_pre"], p["wz"], p["wxbc"], p["wdt"], p["wu"],
                                     p["cw"], p["cb"], p["dtb"], p["alog"], p["dskip"],
                                     tril, colsel[0], headsel[0], nb=nb, seq=seq, tm=tm_ssd)
        x = _ssd_bwd(xc, dt, yf, z, u, x, p["dtb"], p["alog"], triu, colsel[1], headsel[1],
                     p["normw"], dmat, p["poolw"], p["pscale"], p["wout"], p["g_mix_post"],
                     nb=nb, seq=seq, tm=tm_ssd)
        x = _mlp(x, p["g_mlp_pre"], p["w1"], p["w2"], p["g_mlp_post"], tm=tm_mlp, tf=tf)
    return x.reshape(nb, seq, D_MODEL)


def kernel(x_prompt, x_sample, g_mix_pre, g_mix_post, g_mlp_pre, g_mlp_post, w_in, conv_w, conv_b, dt_bias, a_log, d_skip, ssm_norm_w, pool_w, pool_scale, w_out, w_mlp_in, w_mlp_out):
    depth = w_in.shape[0]
    layers = [_layer_params(i, g_mix_pre, g_mix_post, g_mlp_pre, g_mlp_post, w_in, conv_w,
                            conv_b, dt_bias, a_log, d_skip, ssm_norm_w, pool_w, pool_scale,
                            w_out, w_mlp_in, w_mlp_out) for i in range(depth)]
    consts = _selectors()
    run = functools.partial(_trunk, layers=layers, consts=consts,
                            tm_ssd=2 * CHUNK, tm_mlp=512, tf=1024)
    return (run(x_prompt), run(x_sample))
```

```python
import functools

import numpy as np
import jax
import jax.numpy as jnp
from jax import lax
from jax.experimental import pallas as pl
from jax.experimental.pallas import tpu as pltpu

D_MODEL = 2048
D_SSM = 1024
HEAD_DIM = 64
N_HEADS = 16
GROUPS = 2
HEADS_PER_GROUP = N_HEADS // GROUPS
GROUP_W = HEADS_PER_GROUP * HEAD_DIM
D_STATE = 128
CONV_K = 5
CHUNK = 128
D_POOL = 1024
POOL_WINDOWS = (2, 4, 8, 16)
POOL_GROUP_DIM = D_POOL // len(POOL_WINDOWS)
CONV_DIM = D_SSM + 2 * GROUPS * D_STATE
D_FF = 4 * D_MODEL
NORM_EPS = 1e-6

LANES = 128
SUBLANES = 8
HALO = SUBLANES
POOL_HALO = 2 * SUBLANES
DT_PAD = LANES
DT_COLS = 2 * N_HEADS
DT_REP = 3
HEADS_PER_DOT = 4
POOL_K = 2 * CHUNK
MXU_N = 256
MLP_ROWS = 256
FWD_LAG = 2
FWD_FILL_CONV = 1
FWD_FILL_PLAN = (1, 1, 1, 1, 1, 1)
BWD_FILL_PLAN = (1, 1, 1, 1, 1, 1)
RING = 4
NEG_BIG = -1e30
VMEM_LIMIT = 56 * 1024 * 1024

F32 = jnp.float32
BF16 = jnp.bfloat16


def _dot(a, b):
    return jnp.dot(a, b, preferred_element_type=F32)


def _dot_nt(a, b):
    return lax.dot_general(a, b, (((1,), (1,)), ((), ())), preferred_element_type=F32)


def _split3(v):
    hi = v.astype(BF16)
    r1 = v - hi.astype(F32)
    lo = r1.astype(BF16)
    lo2 = (r1 - lo.astype(F32)).astype(BF16)
    return hi, lo, lo2


def _lane_split(v):
    hi, lo, lo2 = _split3(v)
    lane = lax.broadcasted_iota(jnp.int32, v.shape, 1)
    return jnp.where(lane < DT_COLS, hi, jnp.where(lane < 2 * DT_COLS, lo, lo2))


def _exact_rdot(v, sel):
    return _dot(_lane_split(v), sel)


def _exact_cumsum(sel3, v):
    return _dot(sel3, jnp.concatenate(_split3(v), axis=0))


def _silu(v):
    return v * (1.0 / (1.0 + jnp.exp(-v)))


def _softplus(v):
    return jnp.maximum(v, 0.0) + jnp.log1p(jnp.exp(-jnp.abs(v)))


def _rms(v):
    return v * lax.rsqrt(jnp.mean(v * v, axis=-1, keepdims=True) + NORM_EPS)


def _ssd_chunk(xc, dtr, dtb, a_row, cum_sel, colsel, headsel, h_ref, *, reverse, fresh=None,
               fill=lambda: None):
    lane_off = N_HEADS if reverse else 0
    edge = 0 if reverse else CHUNK - 1
    xs = xc[:, :D_SSM].astype(F32)
    xs_b = xc[:, :D_SSM].astype(BF16)

    dt = _softplus(dtr + dtb)
    acum = _exact_cumsum(cum_sel, dt * a_row)
    fill()
    g_t = (acum - jnp.log(dt)).T
    a_col = _exact_rdot(acum, colsel)
    a_edge = acum[edge:edge + 1, :]
    w_exp = _exact_rdot(jnp.exp(a_edge - acum) * dt, headsel)
    e_exp = _exact_rdot(jnp.exp(acum), headsel)
    fill()
    xw_b = (w_exp * xs).astype(BF16)

    q_idx = lax.broadcasted_iota(jnp.int32, (CHUNK, CHUNK), 0)
    s_idx = lax.broadcasted_iota(jnp.int32, (CHUNK, CHUNK), 1)
    allowed = (s_idx >= q_idx) if reverse else (s_idx <= q_idx)
    quad_w = HEADS_PER_DOT * HEAD_DIM
    head_of_lane = lax.broadcasted_iota(jnp.int32, (CHUNK, quad_w), 1) // HEAD_DIM

    pieces = []
    for g in range(GROUPS):
        b_g = xc[:, D_SSM + g * D_STATE:D_SSM + (g + 1) * D_STATE]
        c_g = xc[:, D_SSM + GROUPS * D_STATE + g * D_STATE:
                 D_SSM + GROUPS * D_STATE + (g + 1) * D_STATE].astype(BF16)
        scores = _dot_nt(c_g, b_g.astype(BF16))
        h_prev = h_ref[g]
        if fresh is not None:
            h_prev = jnp.where(fresh, 0.0, h_prev)
        y_off = _dot(c_g, h_prev.astype(BF16))
        for qd in range(HEADS_PER_GROUP // HEADS_PER_DOT):
            h0 = g * HEADS_PER_GROUP + HEADS_PER_DOT * qd
            lo = h0 * HEAD_DIM
            wmats = []
            for hh in range(h0, h0 + HEADS_PER_DOT):
                seg = (a_col[:, hh * CHUNK:(hh + 1) * CHUNK]
                       - g_t[lane_off + hh:lane_off + hh + 1, :])
                wmats.append((scores * jnp.exp(jnp.where(allowed, seg, NEG_BIG))).astype(BF16))
            x_quad = xs_b[:, lo:lo + quad_w]
            x_diag = jnp.concatenate(
                [jnp.where(head_of_lane == k, x_quad, jnp.zeros_like(x_quad))
                 for k in range(HEADS_PER_DOT)], axis=0)
            y_diag = _dot(jnp.concatenate(wmats, axis=1), x_diag)
            fill()
            pieces.append(y_diag + y_off[:, qd * quad_w:(qd + 1) * quad_w]
                          * e_exp[:, lo:lo + quad_w])
        states = _dot(b_g.astype(F32).T.astype(BF16), xw_b[:, g * GROUP_W:(g + 1) * GROUP_W])
        decay = e_exp[edge:edge + 1, g * GROUP_W:(g + 1) * GROUP_W]
        h_ref[g] = decay * h_prev + states
    return jnp.concatenate(pieces, axis=-1)


def _dt_consts(alog_ref):
    lane = lax.broadcasted_iota(jnp.int32, (1, DT_PAD), 1)
    return jnp.where(lane < DT_REP * DT_COLS, -jnp.exp(alog_ref[...]), 0.0)


def _proj_fwd_kernel(x0_ref, xn_ref, g_ref, wz_ref, wxbc_ref, wdt_ref, wu_ref, cw_ref, cb_ref,
                     dtb_ref, alog_ref, dskip_ref, tril_ref, colsel_ref, headsel_ref,
                     z_ref, u_ref, dt_ref, xc_ref, yf_ref,
                     xbc_ring, dt_ring, h_ref, hn_ref, hn_next_ref, xcf_ref, *, tm, nt):
    i = pl.program_id(1)

    @pl.when(i == 0)
    def _():
        xbc_ring[...] = jnp.zeros_like(xbc_ring)
        dt_ring[...] = jnp.zeros_like(dt_ring)
        h_ref[...] = jnp.zeros_like(h_ref)
        hn_next_ref[...] = (_rms(x0_ref[...]) * g_ref[...]).astype(BF16)

    t = i - FWD_LAG
    cur = lax.rem(i + RING - FWD_LAG, RING)
    prv = lax.rem(i + RING - FWD_LAG - 1, RING)
    nxt = lax.rem(i + RING - FWD_LAG + 1, RING)
    head = jnp.where(t > 0, xbc_ring[prv, tm - HALO:tm, :], 0.0)
    tail = jnp.where(t < nt - 1, xbc_ring[nxt, 0:HALO, :], 0.0)
    nchunk = tm // CHUNK
    mid = CONV_K // 2
    rows = CHUNK + 2 * HALO
    slot = lax.rem(i, RING)
    a_row = _dt_consts(alog_ref)
    dtb = dtb_ref[...]
    dts = dt_ring[cur]
    hn_ref[...] = hn_next_ref[...]

    def conv(c):
        r0 = c * CHUNK
        blk = jnp.concatenate([
            head if c == 0 else xbc_ring[cur, r0 - HALO:r0, :],
            xbc_ring[cur, r0:r0 + CHUNK, :],
            tail if c == nchunk - 1 else xbc_ring[cur, r0 + CHUNK:r0 + CHUNK + HALO, :],
        ], axis=0)
        acc = cb_ref[...] + cw_ref[mid:mid + 1, :] * blk[HALO:HALO + CHUNK, :]
        for k in range(CONV_K):
            if k != mid:
                rolled = pltpu.roll(blk, rows - (HALO - mid + k), axis=0)
                acc = acc + cw_ref[k:k + 1, :] * rolled[0:CHUNK, :]
        xc = _silu(acc)
        xcf_ref[r0:r0 + CHUNK, :] = xc
        xc_ref[r0:r0 + CHUNK, :] = xc.astype(BF16)

    def scan(c):
        r0 = c * CHUNK
        xc = xcf_ref[r0:r0 + CHUNK, :]
        y = _ssd_chunk(xc, dts[r0:r0 + CHUNK, :], dtb, a_row, tril_ref[...], colsel_ref[...],
                       headsel_ref[...], h_ref, reverse=False,
                       fresh=(t <= 0) if c == 0 else None, fill=fill)
        yf_ref[r0:r0 + CHUNK, :] = y + dskip_ref[...] * xc[:, :D_SSM]

    def proj_z(lo):
        z_ref[:, lo:lo + MXU_N] = _dot(hn_ref[...], wz_ref[:, lo:lo + MXU_N])

    def proj_u(lo):
        u_ref[:, lo:lo + MXU_N] = _dot(hn_ref[...], wu_ref[:, lo:lo + MXU_N]).astype(BF16)

    def proj_dt():
        dtr = _dot(hn_ref[...], wdt_ref[...])
        dt_ref[...] = dtr
        dt_ring[slot] = dtr

    def proj_xbc(lo):
        xbc_ring[slot, :, lo:lo + MXU_N] = _dot(hn_ref[...], wxbc_ref[:, lo:lo + MXU_N])

    pieces = ([functools.partial(proj_z, lo) for lo in range(0, D_SSM, MXU_N)]
              + [functools.partial(proj_u, lo) for lo in range(0, D_POOL, MXU_N)]
              + [proj_dt]
              + [functools.partial(proj_xbc, lo) for lo in range(0, CONV_DIM, MXU_N)])
    queue = iter(pieces)

    def emit(n):
        for _ in range(n):
            piece = next(queue, None)
            if piece is not None:
                piece()

    plan = iter(FWD_FILL_PLAN * nchunk)

    def fill():
        emit(next(plan, 0))

    for c in range(nchunk):
        emit(FWD_FILL_CONV)
        conv(c)
    for c in range(nchunk):
        scan(c)
    emit(len(pieces))
    hn_next_ref[...] = (_rms(xn_ref[...]) * g_ref[...]).astype(BF16)


def _proj_fwd(x, g, wz, wxbc, wdt, wu, cw, cb, dtb, alog, dskip, tril, colsel, headsel,
              *, nb, seq, tm):
    nt = seq // tm
    proj_row = lambda b, i: (b * nt + jnp.minimum(i, nt - 1), 0)
    norm_row = lambda b, i: (b * nt + jnp.minimum(i + 1, nt - 1), 0)
    scan_row = lambda b, i: (b * nt + jnp.clip(i - FWD_LAG, 0, nt - 1), 0)
    const = functools.partial(pl.BlockSpec, index_map=lambda b, i: (0, 0),
                              pipeline_mode=pl.Buffered(1))
    t = nb * seq
    return pl.pallas_call(
        functools.partial(_proj_fwd_kernel, tm=tm, nt=nt),
        grid=(nb, nt + FWD_LAG),
        in_specs=[
            pl.BlockSpec((tm, D_MODEL), lambda b, i: (b * nt, 0)),
            pl.BlockSpec((tm, D_MODEL), norm_row),
            const((1, D_MODEL)),
            const((D_MODEL, D_SSM)),
            const((D_MODEL, CONV_DIM)),
            const((D_MODEL, DT_PAD)),
            const((D_MODEL, D_POOL)),
            const((SUBLANES, CONV_DIM)),
            const((1, CONV_DIM)),
            const((1, DT_PAD)),
            const((1, DT_PAD)),
            const((1, D_SSM)),
            const((CHUNK, DT_REP * CHUNK)),
            const((DT_PAD, N_HEADS * CHUNK)),
            const((DT_PAD, D_SSM)),
        ],
        out_specs=[
            pl.BlockSpec((tm, D_SSM), proj_row),
            pl.BlockSpec((tm, D_POOL), proj_row),
            pl.BlockSpec((tm, DT_PAD), proj_row),
            pl.BlockSpec((tm, CONV_DIM), scan_row),
            pl.BlockSpec((tm, D_SSM), scan_row),
        ],
        out_shape=[
            jax.ShapeDtypeStruct((t, D_SSM), F32),
            jax.ShapeDtypeStruct((t, D_POOL), BF16),
            jax.ShapeDtypeStruct((t, DT_PAD), F32),
            jax.ShapeDtypeStruct((t, CONV_DIM), BF16),
            jax.ShapeDtypeStruct((t, D_SSM), F32),
        ],
        scratch_shapes=[
            pltpu.VMEM((RING, tm, CONV_DIM), F32),
            pltpu.VMEM((RING, tm, DT_PAD), F32),
            pltpu.VMEM((GROUPS, D_STATE, GROUP_W), F32),
            pltpu.VMEM((tm, D_MODEL), BF16),
            pltpu.VMEM((tm, D_MODEL), BF16),
            pltpu.VMEM((tm, CONV_DIM), F32),
        ],
        compiler_params=pltpu.CompilerParams(
            dimension_semantics=("arbitrary", "arbitrary"), vmem_limit_bytes=VMEM_LIMIT),
        name="proj_ssd_fwd",
    )(x, x, g, wz, wxbc, wdt, wu, cw, cb, dtb, alog, dskip, tril, colsel, headsel)


def _ssd_bwd_kernel(xc_ref, dt_ref, yf_ref, z_ref, u_ref, uprev_ref, unext_ref, x_ref,
                    dtb_ref, alog_ref, triu_ref, colsel_ref, headsel_ref, normw_ref,
                    dmat_ref, poolw_ref, pscale_ref, wout_ref, gpost_ref,
                    o_ref, y_ref, extu_ref, mixin_ring, mix_ref, h_ref, *, tm, nt):
    j = pl.program_id(1)
    tile = nt - 1 - jnp.minimum(j, nt - 1)

    @pl.when(j == 0)
    def _():
        h_ref[...] = jnp.zeros_like(h_ref)
        mixin_ring[...] = jnp.zeros_like(mixin_ring)

    cur = lax.rem(j, 2)
    prv = 1 - cur

    def out_proj(lo):
        mix_ref[:, lo:lo + MXU_N] = _dot(mixin_ring[prv], wout_ref[:, lo:lo + MXU_N])

    pieces = [functools.partial(out_proj, lo) for lo in range(0, D_MODEL, MXU_N)]
    queue = iter(pieces)

    def emit(n):
        for _ in range(n):
            piece = next(queue, None)
            if piece is not None:
                piece()

    nchunk = tm // CHUNK
    plan = iter(BWD_FILL_PLAN * nchunk)

    def fill():
        emit(next(plan, 0))

    a_row = _dt_consts(alog_ref)
    dtb = dtb_ref[...]
    for c in reversed(range(nchunk)):
        r0 = c * CHUNK
        y = _ssd_chunk(xc_ref[r0:r0 + CHUNK, :], dt_ref[r0:r0 + CHUNK, :], dtb, a_row,
                       triu_ref[...], colsel_ref[...], headsel_ref[...], h_ref, reverse=True,
                       fill=fill)
        y_ref[r0:r0 + CHUNK, :] = y + yf_ref[r0:r0 + CHUNK, :]
    emit(len(pieces))

    v = y_ref[...] * _silu(z_ref[...])
    for g in range(GROUPS):
        lo = g * GROUP_W
        mixin_ring[cur, :, lo:lo + GROUP_W] = (_rms(v[:, lo:lo + GROUP_W])
                                               * normw_ref[:, lo:lo + GROUP_W]).astype(BF16)

    ph = POOL_HALO
    extu_ref[0:ph, :] = jnp.where(tile > 0, uprev_ref[...], jnp.zeros_like(uprev_ref))
    extu_ref[ph:ph + tm, :] = u_ref[...]
    extu_ref[ph + tm:2 * ph + tm, :] = jnp.where(tile < nt - 1, unext_ref[...],
                                                 jnp.zeros_like(unext_ref))
    extu_ref[2 * ph + tm:, :] = jnp.zeros((POOL_K - CHUNK - 2 * ph, D_POOL), BF16)
    for c in range(nchunk):
        r0 = c * CHUNK
        first = jnp.logical_and(tile == 0, c == 0)
        final = jnp.logical_and(tile == nt - 1, c == nchunk - 1)
        variant = jnp.where(first, 0, jnp.where(final, 2, 1))
        for gi in range(len(POOL_WINDOWS)):
            lo = gi * POOL_GROUP_DIM
            band = dmat_ref[variant * len(POOL_WINDOWS) + gi]
            d = _dot(band, extu_ref[r0:r0 + POOL_K, lo:lo + POOL_GROUP_DIM]).astype(BF16)
            pooled = _dot(d, poolw_ref[gi]) * pscale_ref[:, lo:lo + POOL_GROUP_DIM]
            mixin_ring[cur, r0:r0 + CHUNK,
                       D_SSM + lo:D_SSM + lo + POOL_GROUP_DIM] = pooled.astype(BF16)

    o_ref[...] = x_ref[...] + _rms(mix_ref[...]) * gpost_ref[...]


def _ssd_bwd(xc, dt, yf, z, u, x, dtb, alog, triu, colsel, headsel, normw, dmat, poolw, pscale,
             wout, gpost, *, nb, seq, tm):
    nt = seq // tm
    hb = tm // POOL_HALO
    nhalo = nb * seq // POOL_HALO
    tile = lambda b, j: b * nt + nt - 1 - jnp.minimum(j, nt - 1)
    row = lambda b, j: (tile(b, j), 0)
    prev = lambda b, j: (jnp.maximum(tile(b, j) * hb - 1, 0), 0)
    nxt = lambda b, j: (jnp.minimum((tile(b, j) + 1) * hb, nhalo - 1), 0)
    out_row = lambda b, j: (b * nt + jnp.minimum(nt - j, nt - 1), 0)
    const2 = functools.partial(pl.BlockSpec, index_map=lambda b, j: (0, 0),
                               pipeline_mode=pl.Buffered(1))
    const3 = functools.partial(pl.BlockSpec, index_map=lambda b, j: (0, 0, 0),
                               pipeline_mode=pl.Buffered(1))
    t = nb * seq
    nvar = 3 * len(POOL_WINDOWS)
    return pl.pallas_call(
        functools.partial(_ssd_bwd_kernel, tm=tm, nt=nt),
        grid=(nb, nt + 1),
        in_specs=[
            pl.BlockSpec((tm, CONV_DIM), row),
            pl.BlockSpec((tm, DT_PAD), row),
            pl.BlockSpec((tm, D_SSM), row),
            pl.BlockSpec((tm, D_SSM), row),
            pl.BlockSpec((tm, D_POOL), row),
            pl.BlockSpec((POOL_HALO, D_POOL), prev),
            pl.BlockSpec((POOL_HALO, D_POOL), nxt),
            pl.BlockSpec((tm, D_MODEL), out_row),
            const2((1, DT_PAD)),
            const2((1, DT_PAD)),
            const2((CHUNK, DT_REP * CHUNK)),
            const2((DT_PAD, N_HEADS * CHUNK)),
            const2((DT_PAD, D_SSM)),
            const2((1, D_SSM)),
            const3((nvar, CHUNK, POOL_K)),
            const3((len(POOL_WINDOWS), POOL_GROUP_DIM, POOL_GROUP_DIM)),
            const2((1, D_POOL)),
            const2((D_SSM + D_POOL, D_MODEL)),
            const2((1, D_MODEL)),
        ],
        out_specs=pl.BlockSpec((tm, D_MODEL), out_row),
        out_shape=jax.ShapeDtypeStruct((t, D_MODEL), F32),
        scratch_shapes=[
            pltpu.VMEM((tm, D_SSM), F32),
            pltpu.VMEM((tm + CHUNK, D_POOL), BF16),
            pltpu.VMEM((2, tm, D_SSM + D_POOL), BF16),
            pltpu.VMEM((tm, D_MODEL), F32),
            pltpu.VMEM((GROUPS, D_STATE, GROUP_W), F32),
        ],
        compiler_params=pltpu.CompilerParams(
            dimension_semantics=("arbitrary", "arbitrary"), vmem_limit_bytes=VMEM_LIMIT),
        name="ssd_bwd_mix",
    )(xc, dt, yf, z, u, u, u, x, dtb, alog, triu, colsel, headsel, normw, dmat, poolw, pscale,
      wout, gpost)


def _mlp_kernel(x_ref, gpre_ref, w1_ref, w2_ref, gpost_ref, o_ref, h_ref, acc_ref, act_ref):
    f = pl.program_id(1)
    last = pl.num_programs(1) - 1
    blocks = [slice(r, r + MLP_ROWS) for r in range(0, x_ref.shape[0], MLP_ROWS)]

    def sq_relu(a):
        a = jnp.maximum(a, 0.0)
        return (a * a).astype(BF16)

    @pl.when(f == 0)
    def _():
        for rs in blocks:
            h_ref[rs, :] = (_rms(x_ref[rs, :]) * gpre_ref[...]).astype(BF16)
            act_ref[rs, :] = sq_relu(_dot(h_ref[rs, :], w1_ref[...]))
        acc_ref[...] = _dot(act_ref[...], w2_ref[...])

    @pl.when(jnp.logical_and(f > 0, f < last))
    def _():
        acc_ref[...] += _dot(sq_relu(_dot(h_ref[...], w1_ref[...])), w2_ref[...])

    @pl.when(f == last)
    def _():
        act_ref[...] = sq_relu(_dot(h_ref[...], w1_ref[...]))
        for rs in blocks:
            m = acc_ref[rs, :] + _dot(act_ref[rs, :], w2_ref[...])
            o_ref[rs, :] = x_ref[rs, :] + _rms(m) * gpost_ref[...]


def _mlp(x, gpre, w1, w2, gpost, *, tm, tf):
    t = x.shape[0]
    row = lambda i, f: (i, 0)
    fixed = lambda i, f: (0, 0)
    const = functools.partial(pl.BlockSpec, index_map=fixed, pipeline_mode=pl.Buffered(1))
    return pl.pallas_call(
        _mlp_kernel,
        grid=(t // tm, D_FF // tf),
        in_specs=[
            pl.BlockSpec((tm, D_MODEL), row),
            const((1, D_MODEL)),
            pl.BlockSpec((D_MODEL, tf), lambda i, f: (0, f)),
            pl.BlockSpec((tf, D_MODEL), lambda i, f: (f, 0)),
            const((1, D_MODEL)),
        ],
        out_specs=pl.BlockSpec((tm, D_MODEL), row),
        out_shape=jax.ShapeDtypeStruct((t, D_MODEL), F32),
        scratch_shapes=[
            pltpu.VMEM((tm, D_MODEL), BF16),
            pltpu.VMEM((tm, D_MODEL), F32),
            pltpu.VMEM((tm, tf), BF16),
        ],
        compiler_params=pltpu.CompilerParams(
            dimension_semantics=("arbitrary", "arbitrary"), vmem_limit_bytes=VMEM_LIMIT),
        name="mlp",
    )(x, gpre, w1, w2, gpost)


def _selectors():
    q = np.arange(CHUNK)[:, None]
    s = np.arange(CHUNK)[None, :]
    tril = np.tile((s <= q).astype(np.float32), (1, DT_REP))
    triu = np.tile((s >= q).astype(np.float32), (1, DT_REP))
    colsel = np.zeros((2, DT_PAD, N_HEADS * CHUNK), np.float32)
    headsel = np.zeros((2, DT_PAD, D_SSM), np.float32)
    for d in range(2):
        for h in range(N_HEADS):
            for rep in range(DT_REP):
                src = rep * DT_COLS + d * N_HEADS + h
                colsel[d, src, h * CHUNK:(h + 1) * CHUNK] = 1.0
                headsel[d, src, h * HEAD_DIM:(h + 1) * HEAD_DIM] = 1.0
    as_bf16 = lambda a: jnp.asarray(a, dtype=BF16)
    return as_bf16(tril), as_bf16(triu), as_bf16(colsel), as_bf16(headsel)


def _pool_bands(seq):
    out = np.zeros((3, len(POOL_WINDOWS), CHUNK, POOL_K), np.float32)
    for v, start in enumerate((0, CHUNK, seq - CHUNK)):
        t = start + np.arange(CHUNK)[:, None]
        src = start + np.arange(POOL_K)[None, :] - POOL_HALO
        for gi, w in enumerate(POOL_WINDOWS):
            hi = np.minimum(t + w // 2, seq)
            lo = np.maximum(t - w // 2, 0)
            inside = (src >= lo) & (src < hi)
            out[v, gi] = inside / (hi - lo).astype(np.float32) - (src == t)
    return jnp.asarray(out.reshape(3 * len(POOL_WINDOWS), CHUNK, POOL_K), dtype=BF16)


def _layer_params(i, g_mix_pre, g_mix_post, g_mlp_pre, g_mlp_post, w_in, conv_w, conv_b,
                  dt_bias, a_log, d_skip, ssm_norm_w, pool_w, pool_scale, w_out,
                  w_mlp_in, w_mlp_out):
    row = lambda v: v.reshape(1, -1).astype(F32)
    rep_lanes = lambda v: jnp.pad(jnp.tile(v.astype(F32).reshape(-1, DT_COLS), (1, DT_REP)),
                                  ((0, 0), (0, DT_PAD - DT_REP * DT_COLS)))
    o_dt = D_SSM + CONV_DIM
    w = w_in[i]
    return dict(
        g_mix_pre=row(g_mix_pre[i]), g_mix_post=row(g_mix_post[i]),
        g_mlp_pre=row(g_mlp_pre[i]), g_mlp_post=row(g_mlp_post[i]),
        wz=w[:, :D_SSM].astype(BF16),
        wxbc=w[:, D_SSM:o_dt].astype(BF16),
        wdt=rep_lanes(w[:, o_dt:o_dt + DT_COLS]).astype(BF16),
        wu=w[:, o_dt + DT_COLS:].astype(BF16),
        cw=jnp.pad(conv_w[i].astype(F32), ((0, SUBLANES - CONV_K), (0, 0))),
        cb=row(conv_b[i]),
        dtb=rep_lanes(dt_bias[i]), alog=rep_lanes(a_log[i]),
        dskip=row(jnp.repeat(d_skip[i], HEAD_DIM)),
        normw=row(ssm_norm_w[i]),
        poolw=pool_w[i].astype(BF16), pscale=row(pool_scale[i]),
        wout=w_out[i].astype(BF16),
        w1=w_mlp_in[i].astype(BF16), w2=w_mlp_out[i].astype(BF16),
    )


def _trunk(x, layers, consts, *, tm_ssd, tm_mlp, tf):
    nb, seq, _ = x.shape
    tril, triu, colsel, headsel = consts
    dmat = _pool_bands(seq)
    x = x.reshape(nb * seq, D_MODEL)
    for p in layers:
        z, u, dt, xc, yf = _proj_fwd(x, p["g_mix_pre"], p["wz"], p["wxbc"], p["wdt"], p["wu"],
                                     p["cw"], p["cb"], p["dtb"], p["alog"], p["dskip"],
                                     tril, colsel[0], headsel[0], nb=nb, seq=seq, tm=tm_ssd)
        x = _ssd_bwd(xc, dt, yf, z, u, x, p["dtb"], p["alog"], triu, colsel[1], headsel[1],
                     p["normw"], dmat, p["poolw"], p["pscale"], p["wout"], p["g_mix_post"],
                     nb=nb, seq=seq, tm=tm_ssd)
        x = _mlp(x, p["g_mlp_pre"], p["w1"], p["w2"], p["g_mlp_post"], tm=tm_mlp, tf=tf)
    return x.reshape(nb, seq, D_MODEL)


def kernel(x_prompt, x_sample, g_mix_pre, g_mix_post, g_mlp_pre, g_mlp_post, w_in, conv_w, conv_b, dt_bias, a_log, d_skip, ssm_norm_w, pool_w, pool_scale, w_out, w_mlp_in, w_mlp_out):
    depth = w_in.shape[0]
    layers = [_layer_params(i, g_mix_pre, g_mix_post, g_mlp_pre, g_mlp_post, w_in, conv_w,
                            conv_b, dt_bias, a_log, d_skip, ssm_norm_w, pool_w, pool_scale,
                            w_out, w_mlp_in, w_mlp_out) for i in range(depth)]
    consts = _selectors()
    run = functools.partial(_trunk, layers=layers, consts=consts,
                            tm_ssd=2 * CHUNK, tm_mlp=512, tf=1024)
    return (run(x_prompt), run(x_sample))
```

```python
import functools

import numpy as np
import jax
import jax.numpy as jnp
from jax import lax
from jax.experimental import pallas as pl
from jax.experimental.pallas import tpu as pltpu

D_MODEL = 2048
D_SSM = 1024
HEAD_DIM = 64
N_HEADS = 16
GROUPS = 2
HEADS_PER_GROUP = N_HEADS // GROUPS
GROUP_W = HEADS_PER_GROUP * HEAD_DIM
D_STATE = 128
CONV_K = 5
CHUNK = 128
D_POOL = 1024
POOL_WINDOWS = (2, 4, 8, 16)
POOL_GROUP_DIM = D_POOL // len(POOL_WINDOWS)
CONV_DIM = D_SSM + 2 * GROUPS * D_STATE
D_FF = 4 * D_MODEL
NORM_EPS = 1e-6

LANES = 128
SUBLANES = 8
HALO = SUBLANES
POOL_HALO = 2 * SUBLANES
DT_PAD = LANES
DT_COLS = 2 * N_HEADS
DT_REP = 3
HEADS_PER_DOT = 4
POOL_K = 2 * CHUNK
MXU_N = 256
SSD_TM = 2 * CHUNK
MLP_TM = 512
MLP_TF = 1024
MLP_ROWS = 256
FWD_LAG = 2
FWD_FILL_CONV = 1
FWD_FILL_PLAN = (1, 1, 1, 1, 1, 1)
BWD_FILL_PLAN = (1, 1, 1, 1, 1, 1)
RING = 4
NEG_BIG = -1e30
VMEM_LIMIT = 56 * 1024 * 1024

F32 = jnp.float32
BF16 = jnp.bfloat16


def _dot(a, b):
    return jnp.dot(a, b, preferred_element_type=F32)


def _dot_nt(a, b):
    return lax.dot_general(a, b, (((1,), (1,)), ((), ())), preferred_element_type=F32)


def _split3(v):
    hi = v.astype(BF16)
    r1 = v - hi.astype(F32)
    lo = r1.astype(BF16)
    lo2 = (r1 - lo.astype(F32)).astype(BF16)
    return hi, lo, lo2


def _lane_split(v):
    hi, lo, lo2 = _split3(v)
    lane = lax.broadcasted_iota(jnp.int32, v.shape, 1)
    return jnp.where(lane < DT_COLS, hi, jnp.where(lane < 2 * DT_COLS, lo, lo2))


def _exact_rdot(v, sel):
    return _dot(_lane_split(v), sel)


def _exact_cumsum(sel3, v):
    return _dot(sel3, jnp.concatenate(_split3(v), axis=0))


def _silu(v):
    return v * (1.0 / (1.0 + jnp.exp(-v)))


def _softplus(v):
    return jnp.maximum(v, 0.0) + jnp.log1p(jnp.exp(-jnp.abs(v)))


def _rms(v):
    return v * lax.rsqrt(jnp.mean(v * v, axis=-1, keepdims=True) + NORM_EPS)


def _ssd_chunk(xc, dtr, dtb, a_row, cum_sel, colsel, headsel, h_ref, *, reverse, fresh=None,
               fill=lambda: None):
    lane_off = N_HEADS if reverse else 0
    edge = 0 if reverse else CHUNK - 1
    xs = xc[:, :D_SSM].astype(F32)
    xs_b = xc[:, :D_SSM].astype(BF16)

    dt = _softplus(dtr + dtb)
    acum = _exact_cumsum(cum_sel, dt * a_row)
    fill()
    g_t = (acum - jnp.log(dt)).T
    a_col = _exact_rdot(acum, colsel)
    a_edge = acum[edge:edge + 1, :]
    w_exp = _exact_rdot(jnp.exp(a_edge - acum) * dt, headsel)
    e_exp = _exact_rdot(jnp.exp(acum), headsel)
    fill()
    xw_b = (w_exp * xs).astype(BF16)

    q_idx = lax.broadcasted_iota(jnp.int32, (CHUNK, CHUNK), 0)
    s_idx = lax.broadcasted_iota(jnp.int32, (CHUNK, CHUNK), 1)
    allowed = (s_idx >= q_idx) if reverse else (s_idx <= q_idx)
    quad_w = HEADS_PER_DOT * HEAD_DIM
    head_of_lane = lax.broadcasted_iota(jnp.int32, (CHUNK, quad_w), 1) // HEAD_DIM

    pieces = []
    for g in range(GROUPS):
        b_g = xc[:, D_SSM + g * D_STATE:D_SSM + (g + 1) * D_STATE]
        c_g = xc[:, D_SSM + GROUPS * D_STATE + g * D_STATE:
                 D_SSM + GROUPS * D_STATE + (g + 1) * D_STATE].astype(BF16)
        scores = _dot_nt(c_g, b_g.astype(BF16))
        h_prev = h_ref[g]
        if fresh is not None:
            h_prev = jnp.where(fresh, 0.0, h_prev)
        y_off = _dot(c_g, h_prev.astype(BF16))
        for qd in range(HEADS_PER_GROUP // HEADS_PER_DOT):
            h0 = g * HEADS_PER_GROUP + HEADS_PER_DOT * qd
            lo = h0 * HEAD_DIM
            wmats = []
            for hh in range(h0, h0 + HEADS_PER_DOT):
                seg = (a_col[:, hh * CHUNK:(hh + 1) * CHUNK]
                       - g_t[lane_off + hh:lane_off + hh + 1, :])
                wmats.append((scores * jnp.exp(jnp.where(allowed, seg, NEG_BIG))).astype(BF16))
            x_quad = xs_b[:, lo:lo + quad_w]
            x_diag = jnp.concatenate(
                [jnp.where(head_of_lane == k, x_quad, jnp.zeros_like(x_quad))
                 for k in range(HEADS_PER_DOT)], axis=0)
            y_diag = _dot(jnp.concatenate(wmats, axis=1), x_diag)
            fill()
            pieces.append(y_diag + y_off[:, qd * quad_w:(qd + 1) * quad_w]
                          * e_exp[:, lo:lo + quad_w])
        states = _dot(b_g.astype(F32).T.astype(BF16), xw_b[:, g * GROUP_W:(g + 1) * GROUP_W])
        decay = e_exp[edge:edge + 1, g * GROUP_W:(g + 1) * GROUP_W]
        h_ref[g] = decay * h_prev + states
    return jnp.concatenate(pieces, axis=-1)


def _dt_consts(alog_ref):
    lane = lax.broadcasted_iota(jnp.int32, (1, DT_PAD), 1)
    return jnp.where(lane < DT_REP * DT_COLS, -jnp.exp(alog_ref[...]), 0.0)


def _proj_fwd_kernel(x0_ref, xn_ref, g_ref, wz_ref, wxbc_ref, wdt_ref, wu_ref, cw_ref, cb_ref,
                     dtb_ref, alog_ref, dskip_ref, tril_ref, colsel_ref, headsel_ref,
                     z_ref, u_ref, dt_ref, xc_ref, yf_ref,
                     xbc_ring, dt_ring, h_ref, hn_ref, hn_next_ref, xcf_ref, *, tm, nt):
    i = pl.program_id(0)

    @pl.when(i == 0)
    def _():
        xbc_ring[...] = jnp.zeros_like(xbc_ring)
        dt_ring[...] = jnp.zeros_like(dt_ring)
        h_ref[...] = jnp.zeros_like(h_ref)
        hn_next_ref[...] = (_rms(x0_ref[...]) * g_ref[...]).astype(BF16)

    t = lax.rem(i - FWD_LAG + nt, nt)
    cur = lax.rem(i + RING - FWD_LAG, RING)
    prv = lax.rem(i + RING - FWD_LAG - 1, RING)
    nxt = lax.rem(i + RING - FWD_LAG + 1, RING)
    head = jnp.where(t > 0, xbc_ring[prv, tm - HALO:tm, :], 0.0)
    tail = jnp.where(t < nt - 1, xbc_ring[nxt, 0:HALO, :], 0.0)
    nchunk = tm // CHUNK
    mid = CONV_K // 2
    rows = CHUNK + 2 * HALO
    slot = lax.rem(i, RING)
    a_row = _dt_consts(alog_ref)
    dtb = dtb_ref[...]
    dts = dt_ring[cur]
    hn_ref[...] = hn_next_ref[...]

    def conv(c):
        r0 = c * CHUNK
        blk = jnp.concatenate([
            head if c == 0 else xbc_ring[cur, r0 - HALO:r0, :],
            xbc_ring[cur, r0:r0 + CHUNK, :],
            tail if c == nchunk - 1 else xbc_ring[cur, r0 + CHUNK:r0 + CHUNK + HALO, :],
        ], axis=0)
        acc = cb_ref[...] + cw_ref[mid:mid + 1, :] * blk[HALO:HALO + CHUNK, :]
        for k in range(CONV_K):
            if k != mid:
                rolled = pltpu.roll(blk, rows - (HALO - mid + k), axis=0)
                acc = acc + cw_ref[k:k + 1, :] * rolled[0:CHUNK, :]
        xc = _silu(acc)
        xcf_ref[r0:r0 + CHUNK, :] = xc
        xc_ref[r0:r0 + CHUNK, :] = xc.astype(BF16)

    def scan(c):
        r0 = c * CHUNK
        xc = xcf_ref[r0:r0 + CHUNK, :]
        y = _ssd_chunk(xc, dts[r0:r0 + CHUNK, :], dtb, a_row, tril_ref[...], colsel_ref[...],
                       headsel_ref[...], h_ref, reverse=False,
                       fresh=(t == 0) if c == 0 else None, fill=fill)
        yf_ref[r0:r0 + CHUNK, :] = y + dskip_ref[...] * xc[:, :D_SSM]

    def proj_z(lo):
        z_ref[:, lo:lo + MXU_N] = _dot(hn_ref[...], wz_ref[:, lo:lo + MXU_N])

    def proj_u(lo):
        u_ref[:, lo:lo + MXU_N] = _dot(hn_ref[...], wu_ref[:, lo:lo + MXU_N]).astype(BF16)

    def proj_dt():
        dtr = _dot(hn_ref[...], wdt_ref[...])
        dt_ref[...] = dtr
        dt_ring[slot] = dtr

    def proj_xbc(lo):
        xbc_ring[slot, :, lo:lo + MXU_N] = _dot(hn_ref[...], wxbc_ref[:, lo:lo + MXU_N])

    pieces = ([functools.partial(proj_z, lo) for lo in range(0, D_SSM, MXU_N)]
              + [functools.partial(proj_u, lo) for lo in range(0, D_POOL, MXU_N)]
              + [proj_dt]
              + [functools.partial(proj_xbc, lo) for lo in range(0, CONV_DIM, MXU_N)])
    queue = iter(pieces)

    def emit(n):
        for _ in range(n):
            piece = next(queue, None)
            if piece is not None:
                piece()

    plan = iter(FWD_FILL_PLAN * nchunk)

    def fill():
        emit(next(plan, 0))

    for c in range(nchunk):
        emit(FWD_FILL_CONV)
        conv(c)
    for c in range(nchunk):
        scan(c)
    emit(len(pieces))
    hn_next_ref[...] = (_rms(xn_ref[...]) * g_ref[...]).astype(BF16)


def _proj_fwd(x, g, wz, wxbc, wdt, wu, cw, cb, dtb, alog, dskip, tril, colsel, headsel,
              *, nb, seq, tm):
    nt = seq // tm
    ntot = nb * nt
    proj_row = lambda i: (jnp.minimum(i, ntot - 1), 0)
    norm_row = lambda i: (jnp.minimum(i + 1, ntot - 1), 0)
    scan_row = lambda i: (jnp.clip(i - FWD_LAG, 0, ntot - 1), 0)
    const = functools.partial(pl.BlockSpec, index_map=lambda i: (0, 0),
                              pipeline_mode=pl.Buffered(1))
    t = nb * seq
    return pl.pallas_call(
        functools.partial(_proj_fwd_kernel, tm=tm, nt=nt),
        grid=(ntot + FWD_LAG,),
        in_specs=[
            pl.BlockSpec((tm, D_MODEL), lambda i: (0, 0), pipeline_mode=pl.Buffered(1)),
            pl.BlockSpec((tm, D_MODEL), norm_row),
            const((1, D_MODEL)),
            const((D_MODEL, D_SSM)),
            const((D_MODEL, CONV_DIM)),
            const((D_MODEL, DT_PAD)),
            const((D_MODEL, D_POOL)),
            const((SUBLANES, CONV_DIM)),
            const((1, CONV_DIM)),
            const((1, DT_PAD)),
            const((1, DT_PAD)),
            const((1, D_SSM)),
            const((CHUNK, DT_REP * CHUNK)),
            const((DT_PAD, N_HEADS * CHUNK)),
            const((DT_PAD, D_SSM)),
        ],
        out_specs=[
            pl.BlockSpec((tm, D_SSM), proj_row),
            pl.BlockSpec((tm, D_POOL), proj_row),
            pl.BlockSpec((tm, DT_PAD), proj_row),
            pl.BlockSpec((tm, CONV_DIM), scan_row),
            pl.BlockSpec((tm, D_SSM), scan_row),
        ],
        out_shape=[
            jax.ShapeDtypeStruct((t, D_SSM), F32),
            jax.ShapeDtypeStruct((t, D_POOL), BF16),
            jax.ShapeDtypeStruct((t, DT_PAD), F32),
            jax.ShapeDtypeStruct((t, CONV_DIM), BF16),
            jax.ShapeDtypeStruct((t, D_SSM), F32),
        ],
        scratch_shapes=[
            pltpu.VMEM((RING, tm, CONV_DIM), F32),
            pltpu.VMEM((RING, tm, DT_PAD), F32),
            pltpu.VMEM((GROUPS, D_STATE, GROUP_W), F32),
            pltpu.VMEM((tm, D_MODEL), BF16),
            pltpu.VMEM((tm, D_MODEL), BF16),
            pltpu.VMEM((tm, CONV_DIM), F32),
        ],
        compiler_params=pltpu.CompilerParams(
            dimension_semantics=("arbitrary",), vmem_limit_bytes=VMEM_LIMIT),
        name="proj_ssd_fwd",
    )(x, x, g, wz, wxbc, wdt, wu, cw, cb, dtb, alog, dskip, tril, colsel, headsel)


def _ssd_bwd_kernel(xc_ref, dt_ref, yf_ref, z_ref, u_ref, uprev_ref, unext_ref, x_ref,
                    dtb_ref, alog_ref, triu_ref, colsel_ref, headsel_ref, normw_ref,
                    dmat_ref, poolw_ref, pscale_ref, wout_ref, gpost_ref,
                    o_ref, y_ref, extu_ref, mixin_ring, mix_ref, h_ref, *, tm, nt, ntot):
    j = pl.program_id(0)
    tile = lax.rem(ntot - 1 - jnp.minimum(j, ntot - 1), nt)

    @pl.when(j == 0)
    def _():
        h_ref[...] = jnp.zeros_like(h_ref)
        mixin_ring[...] = jnp.zeros_like(mixin_ring)

    cur = lax.rem(j, 2)
    prv = 1 - cur

    def out_proj(lo):
        mix_ref[:, lo:lo + MXU_N] = _dot(mixin_ring[prv], wout_ref[:, lo:lo + MXU_N])

    pieces = [functools.partial(out_proj, lo) for lo in range(0, D_MODEL, MXU_N)]
    queue = iter(pieces)

    def emit(n):
        for _ in range(n):
            piece = next(queue, None)
            if piece is not None:
                piece()

    nchunk = tm // CHUNK
    plan = iter(BWD_FILL_PLAN * nchunk)

    def fill():
        emit(next(plan, 0))

    a_row = _dt_consts(alog_ref)
    dtb = dtb_ref[...]
    for c in reversed(range(nchunk)):
        r0 = c * CHUNK
        y = _ssd_chunk(xc_ref[r0:r0 + CHUNK, :], dt_ref[r0:r0 + CHUNK, :], dtb, a_row,
                       triu_ref[...], colsel_ref[...], headsel_ref[...], h_ref, reverse=True,
                       fresh=(tile == nt - 1) if c == nchunk - 1 else None, fill=fill)
        y_ref[r0:r0 + CHUNK, :] = y + yf_ref[r0:r0 + CHUNK, :]
    emit(len(pieces))

    v = y_ref[...] * _silu(z_ref[...])
    for g in range(GROUPS):
        lo = g * GROUP_W
        mixin_ring[cur, :, lo:lo + GROUP_W] = (_rms(v[:, lo:lo + GROUP_W])
                                               * normw_ref[:, lo:lo + GROUP_W]).astype(BF16)

    ph = POOL_HALO
    extu_ref[0:ph, :] = jnp.where(tile > 0, uprev_ref[...], jnp.zeros_like(uprev_ref))
    extu_ref[ph:ph + tm, :] = u_ref[...]
    extu_ref[ph + tm:2 * ph + tm, :] = jnp.where(tile < nt - 1, unext_ref[...],
                                                 jnp.zeros_like(unext_ref))
    extu_ref[2 * ph + tm:, :] = jnp.zeros((POOL_K - CHUNK - 2 * ph, D_POOL), BF16)
    for c in range(nchunk):
        r0 = c * CHUNK
        first = jnp.logical_and(tile == 0, c == 0)
        final = jnp.logical_and(tile == nt - 1, c == nchunk - 1)
        variant = jnp.where(first, 0, jnp.where(final, 2, 1))
        for gi in range(len(POOL_WINDOWS)):
            lo = gi * POOL_GROUP_DIM
            band = dmat_ref[variant * len(POOL_WINDOWS) + gi]
            d = _dot(band, extu_ref[r0:r0 + POOL_K, lo:lo + POOL_GROUP_DIM]).astype(BF16)
            pooled = _dot(d, poolw_ref[gi]) * pscale_ref[:, lo:lo + POOL_GROUP_DIM]
            mixin_ring[cur, r0:r0 + CHUNK,
                       D_SSM + lo:D_SSM + lo + POOL_GROUP_DIM] = pooled.astype(BF16)

    o_ref[...] = x_ref[...] + _rms(mix_ref[...]) * gpost_ref[...]


def _ssd_bwd(xc, dt, yf, z, u, x, dtb, alog, triu, colsel, headsel, normw, dmat, poolw, pscale,
             wout, gpost, *, nb, seq, tm):
    nt = seq // tm
    hb = tm // POOL_HALO
    nhalo = nb * seq // POOL_HALO
    ntot = nb * nt
    tile = lambda j: ntot - 1 - jnp.minimum(j, ntot - 1)
    row = lambda j: (tile(j), 0)
    prev = lambda j: (jnp.maximum(tile(j) * hb - 1, 0), 0)
    nxt = lambda j: (jnp.minimum((tile(j) + 1) * hb, nhalo - 1), 0)
    out_row = lambda j: (jnp.minimum(ntot - j, ntot - 1), 0)
    const2 = functools.partial(pl.BlockSpec, index_map=lambda j: (0, 0),
                               pipeline_mode=pl.Buffered(1))
    const3 = functools.partial(pl.BlockSpec, index_map=lambda j: (0, 0, 0),
                               pipeline_mode=pl.Buffered(1))
    t = nb * seq
    nvar = 3 * len(POOL_WINDOWS)
    return pl.pallas_call(
        functools.partial(_ssd_bwd_kernel, tm=tm, nt=nt, ntot=ntot),
        grid=(ntot + 1,),
        in_specs=[
            pl.BlockSpec((tm, CONV_DIM), row),
            pl.BlockSpec((tm, DT_PAD), row),
            pl.BlockSpec((tm, D_SSM), row),
            pl.BlockSpec((tm, D_SSM), row),
            pl.BlockSpec((tm, D_POOL), row),
            pl.BlockSpec((POOL_HALO, D_POOL), prev),
            pl.BlockSpec((POOL_HALO, D_POOL), nxt),
            pl.BlockSpec((tm, D_MODEL), out_row),
            const2((1, DT_PAD)),
            const2((1, DT_PAD)),
            const2((CHUNK, DT_REP * CHUNK)),
            const2((DT_PAD, N_HEADS * CHUNK)),
            const2((DT_PAD, D_SSM)),
            const2((1, D_SSM)),
            const3((nvar, CHUNK, POOL_K)),
            const3((len(POOL_WINDOWS), POOL_GROUP_DIM, POOL_GROUP_DIM)),
            const2((1, D_POOL)),
            const2((D_SSM + D_POOL, D_MODEL)),
            const2((1, D_MODEL)),
        ],
        out_specs=pl.BlockSpec((tm, D_MODEL), out_row),
        out_shape=jax.ShapeDtypeStruct((t, D_MODEL), F32),
        scratch_shapes=[
            pltpu.VMEM((tm, D_SSM), F32),
            pltpu.VMEM((tm + CHUNK, D_POOL), BF16),
            pltpu.VMEM((2, tm, D_SSM + D_POOL), BF16),
            pltpu.VMEM((tm, D_MODEL), F32),
            pltpu.VMEM((GROUPS, D_STATE, GROUP_W), F32),
        ],
        compiler_params=pltpu.CompilerParams(
            dimension_semantics=("arbitrary",), vmem_limit_bytes=VMEM_LIMIT),
        name="ssd_bwd_mix",
    )(xc, dt, yf, z, u, u, u, x, dtb, alog, triu, colsel, headsel, normw, dmat, poolw, pscale,
      wout, gpost)


def _mlp_kernel(x_ref, gpre_ref, w1_ref, w2_ref, gpost_ref, o_ref, h_ref, acc_ref, act_ref):
    f = pl.program_id(1)
    last = pl.num_programs(1) - 1
    blocks = [slice(r, r + MLP_ROWS) for r in range(0, x_ref.shape[0], MLP_ROWS)]

    def sq_relu(a):
        a = jnp.maximum(a, 0.0)
        return (a * a).astype(BF16)

    @pl.when(f == 0)
    def _():
        for rs in blocks:
            h_ref[rs, :] = (_rms(x_ref[rs, :]) * gpre_ref[...]).astype(BF16)
            act_ref[rs, :] = sq_relu(_dot(h_ref[rs, :], w1_ref[...]))
        acc_ref[...] = _dot(act_ref[...], w2_ref[...])

    @pl.when(jnp.logical_and(f > 0, f < last))
    def _():
        acc_ref[...] += _dot(sq_relu(_dot(h_ref[...], w1_ref[...])), w2_ref[...])

    @pl.when(f == last)
    def _():
        act_ref[...] = sq_relu(_dot(h_ref[...], w1_ref[...]))
        for rs in blocks:
            m = acc_ref[rs, :] + _dot(act_ref[rs, :], w2_ref[...])
            o_ref[rs, :] = x_ref[rs, :] + _rms(m) * gpost_ref[...]


def _mlp(x, gpre, w1, w2, gpost, *, tm):
    t = x.shape[0]
    tf = w1.shape[2]
    row = lambda i, f: (i, 0)
    fixed = lambda i, f: (0, 0)
    const = functools.partial(pl.BlockSpec, index_map=fixed, pipeline_mode=pl.Buffered(1))
    return pl.pallas_call(
        _mlp_kernel,
        grid=(t // tm, D_FF // tf),
        in_specs=[
            pl.BlockSpec((tm, D_MODEL), row),
            const((1, D_MODEL)),
            pl.BlockSpec((None, D_MODEL, tf), lambda i, f: (f, 0, 0)),
            pl.BlockSpec((tf, D_MODEL), lambda i, f: (f, 0)),
            const((1, D_MODEL)),
        ],
        out_specs=pl.BlockSpec((tm, D_MODEL), row),
        out_shape=jax.ShapeDtypeStruct((t, D_MODEL), F32),
        scratch_shapes=[
            pltpu.VMEM((tm, D_MODEL), BF16),
            pltpu.VMEM((tm, D_MODEL), F32),
            pltpu.VMEM((tm, tf), BF16),
        ],
        compiler_params=pltpu.CompilerParams(
            dimension_semantics=("arbitrary", "arbitrary"), vmem_limit_bytes=VMEM_LIMIT),
        name="mlp",
    )(x, gpre, w1, w2, gpost)


def _selectors():
    q = np.arange(CHUNK)[:, None]
    s = np.arange(CHUNK)[None, :]
    tril = np.tile((s <= q).astype(np.float32), (1, DT_REP))
    triu = np.tile((s >= q).astype(np.float32), (1, DT_REP))
    colsel = np.zeros((2, DT_PAD, N_HEADS * CHUNK), np.float32)
    headsel = np.zeros((2, DT_PAD, D_SSM), np.float32)
    for d in range(2):
        for h in range(N_HEADS):
            for rep in range(DT_REP):
                src = rep * DT_COLS + d * N_HEADS + h
                colsel[d, src, h * CHUNK:(h + 1) * CHUNK] = 1.0
                headsel[d, src, h * HEAD_DIM:(h + 1) * HEAD_DIM] = 1.0
    as_bf16 = lambda a: jnp.asarray(a, dtype=BF16)
    return as_bf16(tril), as_bf16(triu), as_bf16(colsel), as_bf16(headsel)


def _pool_bands(seq):
    out = np.zeros((3, len(POOL_WINDOWS), CHUNK, POOL_K), np.float32)
    for v, start in enumerate((0, CHUNK, seq - CHUNK)):
        t = start + np.arange(CHUNK)[:, None]
        src = start + np.arange(POOL_K)[None, :] - POOL_HALO
        for gi, w in enumerate(POOL_WINDOWS):
            hi = np.minimum(t + w // 2, seq)
            lo = np.maximum(t - w // 2, 0)
            inside = (src >= lo) & (src < hi)
            out[v, gi] = inside / (hi - lo).astype(np.float32) - (src == t)
    return jnp.asarray(out.reshape(3 * len(POOL_WINDOWS), CHUNK, POOL_K), dtype=BF16)


def _layer_params(i, g_mix_pre, g_mix_post, g_mlp_pre, g_mlp_post, w_in, conv_w, conv_b,
                  dt_bias, a_log, d_skip, ssm_norm_w, pool_w, pool_scale, w_out,
                  w_mlp_in, w_mlp_out):
    row = lambda v: v.reshape(1, -1).astype(F32)
    rep_lanes = lambda v: jnp.pad(jnp.tile(v.astype(F32).reshape(-1, DT_COLS), (1, DT_REP)),
                                  ((0, 0), (0, DT_PAD - DT_REP * DT_COLS)))
    o_dt = D_SSM + CONV_DIM
    w = w_in[i]
    return dict(
        g_mix_pre=row(g_mix_pre[i]), g_mix_post=row(g_mix_post[i]),
        g_mlp_pre=row(g_mlp_pre[i]), g_mlp_post=row(g_mlp_post[i]),
        wz=w[:, :D_SSM].astype(BF16),
        wxbc=w[:, D_SSM:o_dt].astype(BF16),
        wdt=rep_lanes(w[:, o_dt:o_dt + DT_COLS]).astype(BF16),
        wu=w[:, o_dt + DT_COLS:].astype(BF16),
        cw=jnp.pad(conv_w[i].astype(F32), ((0, SUBLANES - CONV_K), (0, 0))),
        cb=row(conv_b[i]),
        dtb=rep_lanes(dt_bias[i]), alog=rep_lanes(a_log[i]),
        dskip=row(jnp.repeat(d_skip[i], HEAD_DIM)),
        normw=row(ssm_norm_w[i]),
        poolw=pool_w[i].astype(BF16), pscale=row(pool_scale[i]),
        wout=w_out[i].astype(BF16),
        w1=w_mlp_in[i].astype(BF16).reshape(D_MODEL, D_FF // MLP_TF, MLP_TF).transpose(1, 0, 2),
        w2=w_mlp_out[i].astype(BF16),
    )


def _trunk(x, layers, consts, *, tm_ssd, tm_mlp):
    nb, seq, _ = x.shape
    tril, triu, colsel, headsel = consts
    dmat = _pool_bands(seq)
    x = x.reshape(nb * seq, D_MODEL)
    for p in layers:
        z, u, dt, xc, yf = _proj_fwd(x, p["g_mix_pre"], p["wz"], p["wxbc"], p["wdt"], p["wu"],
                                     p["cw"], p["cb"], p["dtb"], p["alog"], p["dskip"],
                                     tril, colsel[0], headsel[0], nb=nb, seq=seq, tm=tm_ssd)
        x = _ssd_bwd(xc, dt, yf, z, u, x, p["dtb"], p["alog"], triu, colsel[1], headsel[1],
                     p["normw"], dmat, p["poolw"], p["pscale"], p["wout"], p["g_mix_post"],
                     nb=nb, seq=seq, tm=tm_ssd)
        x = _mlp(x, p["g_mlp_pre"], p["w1"], p["w2"], p["g_mlp_post"], tm=tm_mlp)
    return x.reshape(nb, seq, D_MODEL)


def kernel(x_prompt, x_sample, g_mix_pre, g_mix_post, g_mlp_pre, g_mlp_post, w_in, conv_w, conv_b, dt_bias, a_log, d_skip, ssm_norm_w, pool_w, pool_scale, w_out, w_mlp_in, w_mlp_out):
    depth = w_in.shape[0]
    layers = [_layer_params(i, g_mix_pre, g_mix_post, g_mlp_pre, g_mlp_post, w_in, conv_w,
                            conv_b, dt_bias, a_log, d_skip, ssm_norm_w, pool_w, pool_scale,
                            w_out, w_mlp_in, w_mlp_out) for i in range(depth)]
    consts = _selectors()
    run = functools.partial(_trunk, layers=layers, consts=consts,
                            tm_ssd=SSD_TM, tm_mlp=MLP_TM)
    return (run(x_prompt), run(x_sample))
```

```python
import functools

import numpy as np
import jax
import jax.numpy as jnp
from jax import lax
from jax.experimental import pallas as pl
from jax.experimental.pallas import tpu as pltpu

D_MODEL = 2048
D_SSM = 1024
HEAD_DIM = 64
N_HEADS = 16
GROUPS = 2
HEADS_PER_GROUP = N_HEADS // GROUPS
GROUP_W = HEADS_PER_GROUP * HEAD_DIM
D_STATE = 128
CONV_K = 5
CHUNK = 128
D_POOL = 1024
POOL_WINDOWS = (2, 4, 8, 16)
POOL_GROUP_DIM = D_POOL // len(POOL_WINDOWS)
CONV_DIM = D_SSM + 2 * GROUPS * D_STATE
D_FF = 4 * D_MODEL
NORM_EPS = 1e-6

LANES = 128
SUBLANES = 8
HALO = SUBLANES
POOL_HALO = 2 * SUBLANES
DT_PAD = LANES
DT_COLS = 2 * N_HEADS
DT_REP = 3
HEADS_PER_DOT = 4
POOL_K = 2 * CHUNK
MXU_N = 256
SSD_TM = 2 * CHUNK
MLP_TM = 512
MLP_TF = 1024
MLP_ROWS = 256
FWD_LAG = 2
FWD_FILL_CONV = 1
FWD_FILL_PLAN = (1, 1, 1, 1, 1, 1)
BWD_FILL_PLAN = (1, 1, 1, 1, 1, 1)
RING = 4
NEG_BIG = -1e30
VMEM_LIMIT = 56 * 1024 * 1024

F32 = jnp.float32
BF16 = jnp.bfloat16


def _dot(a, b):
    return jnp.dot(a, b, preferred_element_type=F32)


def _dot_nt(a, b):
    return lax.dot_general(a, b, (((1,), (1,)), ((), ())), preferred_element_type=F32)


def _split3(v):
    hi = v.astype(BF16)
    r1 = v - hi.astype(F32)
    lo = r1.astype(BF16)
    lo2 = (r1 - lo.astype(F32)).astype(BF16)
    return hi, lo, lo2


def _lane_split(v):
    hi, lo, lo2 = _split3(v)
    lane = lax.broadcasted_iota(jnp.int32, v.shape, 1)
    return jnp.where(lane < DT_COLS, hi, jnp.where(lane < 2 * DT_COLS, lo, lo2))


def _exact_rdot(v, sel):
    return _dot(_lane_split(v), sel)


def _exact_cumsum(sel3, v):
    return _dot(sel3, jnp.concatenate(_split3(v), axis=0))


def _silu(v):
    return v * (1.0 / (1.0 + jnp.exp(-v)))


def _softplus(v):
    return jnp.maximum(v, 0.0) + jnp.log1p(jnp.exp(-jnp.abs(v)))


def _rms(v):
    return v * lax.rsqrt(jnp.mean(v * v, axis=-1, keepdims=True) + NORM_EPS)


def _ssd_chunk(xc, dtr, dtb, a_row, cum_sel, colsel, headsel, h_ref, *, reverse, fresh=None,
               fill=lambda: None):
    lane_off = N_HEADS if reverse else 0
    edge = 0 if reverse else CHUNK - 1
    xs = xc[:, :D_SSM].astype(F32)
    xs_b = xc[:, :D_SSM].astype(BF16)

    dt = _softplus(dtr + dtb)
    acum = _exact_cumsum(cum_sel, dt * a_row)
    fill()
    g_t = (acum - jnp.log(dt)).T
    a_col = _exact_rdot(acum, colsel)
    a_edge = acum[edge:edge + 1, :]
    w_exp = _exact_rdot(jnp.exp(a_edge - acum) * dt, headsel)
    e_exp = _exact_rdot(jnp.exp(acum), headsel)
    fill()
    xw_b = (w_exp * xs).astype(BF16)

    q_idx = lax.broadcasted_iota(jnp.int32, (CHUNK, CHUNK), 0)
    s_idx = lax.broadcasted_iota(jnp.int32, (CHUNK, CHUNK), 1)
    allowed = (s_idx >= q_idx) if reverse else (s_idx <= q_idx)
    quad_w = HEADS_PER_DOT * HEAD_DIM
    head_of_lane = lax.broadcasted_iota(jnp.int32, (CHUNK, quad_w), 1) // HEAD_DIM

    pieces = []
    for g in range(GROUPS):
        b_g = xc[:, D_SSM + g * D_STATE:D_SSM + (g + 1) * D_STATE]
        c_g = xc[:, D_SSM + GROUPS * D_STATE + g * D_STATE:
                 D_SSM + GROUPS * D_STATE + (g + 1) * D_STATE].astype(BF16)
        scores = _dot_nt(c_g, b_g.astype(BF16))
        h_prev = h_ref[g]
        if fresh is not None:
            h_prev = jnp.where(fresh, 0.0, h_prev)
        y_off = _dot(c_g, h_prev.astype(BF16))
        for qd in range(HEADS_PER_GROUP // HEADS_PER_DOT):
            h0 = g * HEADS_PER_GROUP + HEADS_PER_DOT * qd
            lo = h0 * HEAD_DIM
            wmats = []
            for hh in range(h0, h0 + HEADS_PER_DOT):
                seg = (a_col[:, hh * CHUNK:(hh + 1) * CHUNK]
                       - g_t[lane_off + hh:lane_off + hh + 1, :])
                wmats.append((scores * jnp.exp(jnp.where(allowed, seg, NEG_BIG))).astype(BF16))
            x_quad = xs_b[:, lo:lo + quad_w]
            x_diag = jnp.concatenate(
                [jnp.where(head_of_lane == k, x_quad, jnp.zeros_like(x_quad))
                 for k in range(HEADS_PER_DOT)], axis=0)
            y_diag = _dot(jnp.concatenate(wmats, axis=1), x_diag)
            fill()
            pieces.append(y_diag + y_off[:, qd * quad_w:(qd + 1) * quad_w]
                          * e_exp[:, lo:lo + quad_w])
        states = _dot(b_g.astype(F32).T.astype(BF16), xw_b[:, g * GROUP_W:(g + 1) * GROUP_W])
        decay = e_exp[edge:edge + 1, g * GROUP_W:(g + 1) * GROUP_W]
        h_ref[g] = decay * h_prev + states
    return jnp.concatenate(pieces, axis=-1)


def _dt_consts(alog_ref):
    lane = lax.broadcasted_iota(jnp.int32, (1, DT_PAD), 1)
    return jnp.where(lane < DT_REP * DT_COLS, -jnp.exp(alog_ref[...]), 0.0)


def _proj_fwd_kernel(x0_ref, xn_ref, g_ref, wz_ref, wxbc_ref, wdt_ref, wu_ref, cw_ref, cb_ref,
                     dtb_ref, alog_ref, dskip_ref, tril_ref, colsel_ref, headsel_ref,
                     z_ref, u_ref, dt_ref, xc_ref, yf_ref,
                     xbc_ring, dt_ring, h_ref, hn_ref, hn_next_ref, xcf_ref, *, tm, nt):
    i = pl.program_id(0)

    @pl.when(i == 0)
    def _():
        xbc_ring[...] = jnp.zeros_like(xbc_ring)
        dt_ring[...] = jnp.zeros_like(dt_ring)
        h_ref[...] = jnp.zeros_like(h_ref)
        hn_next_ref[...] = (_rms(x0_ref[...]) * g_ref[...]).astype(BF16)

    t = lax.rem(i - FWD_LAG + nt, nt)
    cur = lax.rem(i + RING - FWD_LAG, RING)
    prv = lax.rem(i + RING - FWD_LAG - 1, RING)
    nxt = lax.rem(i + RING - FWD_LAG + 1, RING)
    head = jnp.where(t > 0, xbc_ring[prv, tm - HALO:tm, :], 0.0)
    tail = jnp.where(t < nt - 1, xbc_ring[nxt, 0:HALO, :], 0.0)
    nchunk = tm // CHUNK
    mid = CONV_K // 2
    rows = CHUNK + 2 * HALO
    slot = lax.rem(i, RING)
    a_row = _dt_consts(alog_ref)
    dtb = dtb_ref[...]
    dts = dt_ring[cur]
    hn_ref[...] = hn_next_ref[...]

    def conv(c):
        r0 = c * CHUNK
        blk = jnp.concatenate([
            head if c == 0 else xbc_ring[cur, r0 - HALO:r0, :],
            xbc_ring[cur, r0:r0 + CHUNK, :],
            tail if c == nchunk - 1 else xbc_ring[cur, r0 + CHUNK:r0 + CHUNK + HALO, :],
        ], axis=0)
        acc = cb_ref[...] + cw_ref[mid:mid + 1, :] * blk[HALO:HALO + CHUNK, :]
        for k in range(CONV_K):
            if k != mid:
                rolled = pltpu.roll(blk, rows - (HALO - mid + k), axis=0)
                acc = acc + cw_ref[k:k + 1, :] * rolled[0:CHUNK, :]
        xc = _silu(acc)
        xcf_ref[r0:r0 + CHUNK, :] = xc
        xc_ref[r0:r0 + CHUNK, :] = xc.astype(BF16)

    def scan(c):
        r0 = c * CHUNK
        xc = xcf_ref[r0:r0 + CHUNK, :]
        y = _ssd_chunk(xc, dts[r0:r0 + CHUNK, :], dtb, a_row, tril_ref[...], colsel_ref[...],
                       headsel_ref[...], h_ref, reverse=False,
                       fresh=(t == 0) if c == 0 else None, fill=fill)
        yf_ref[r0:r0 + CHUNK, :] = y + dskip_ref[...] * xc[:, :D_SSM]

    def proj_z(lo):
        z_ref[:, lo:lo + MXU_N] = _dot(hn_ref[...], wz_ref[:, lo:lo + MXU_N])

    def proj_u(lo):
        u_ref[:, lo:lo + MXU_N] = _dot(hn_ref[...], wu_ref[:, lo:lo + MXU_N]).astype(BF16)

    def proj_dt():
        dtr = _dot(hn_ref[...], wdt_ref[...])
        dt_ref[...] = dtr
        dt_ring[slot] = dtr

    def proj_xbc(lo):
        xbc_ring[slot, :, lo:lo + MXU_N] = _dot(hn_ref[...], wxbc_ref[:, lo:lo + MXU_N])

    pieces = ([functools.partial(proj_z, lo) for lo in range(0, D_SSM, MXU_N)]
              + [functools.partial(proj_u, lo) for lo in range(0, D_POOL, MXU_N)]
              + [proj_dt]
              + [functools.partial(proj_xbc, lo) for lo in range(0, CONV_DIM, MXU_N)])
    queue = iter(pieces)

    def emit(n):
        for _ in range(n):
            piece = next(queue, None)
            if piece is not None:
                piece()

    plan = iter(FWD_FILL_PLAN * nchunk)

    def fill():
        emit(next(plan, 0))

    for c in range(nchunk):
        emit(FWD_FILL_CONV)
        conv(c)
    for c in range(nchunk):
        scan(c)
    emit(len(pieces))
    hn_next_ref[...] = (_rms(xn_ref[...]) * g_ref[...]).astype(BF16)


def _proj_fwd(x, g, wz, wxbc, wdt, wu, cw, cb, dtb, alog, dskip, tril, colsel, headsel,
              *, nb, seq, tm):
    nt = seq // tm
    ntot = nb * nt
    proj_row = lambda i: (jnp.minimum(i, ntot - 1), 0)
    norm_row = lambda i: (jnp.minimum(i + 1, ntot - 1), 0)
    scan_row = lambda i: (jnp.clip(i - FWD_LAG, 0, ntot - 1), 0)
    const = functools.partial(pl.BlockSpec, index_map=lambda i: (0, 0),
                              pipeline_mode=pl.Buffered(1))
    t = nb * seq
    return pl.pallas_call(
        functools.partial(_proj_fwd_kernel, tm=tm, nt=nt),
        grid=(ntot + FWD_LAG,),
        in_specs=[
            pl.BlockSpec((tm, D_MODEL), lambda i: (0, 0), pipeline_mode=pl.Buffered(1)),
            pl.BlockSpec((tm, D_MODEL), norm_row),
            const((1, D_MODEL)),
            const((D_MODEL, D_SSM)),
            const((D_MODEL, CONV_DIM)),
            const((D_MODEL, DT_PAD)),
            const((D_MODEL, D_POOL)),
            const((SUBLANES, CONV_DIM)),
            const((1, CONV_DIM)),
            const((1, DT_PAD)),
            const((1, DT_PAD)),
            const((1, D_SSM)),
            const((CHUNK, DT_REP * CHUNK)),
            const((DT_PAD, N_HEADS * CHUNK)),
            const((DT_PAD, D_SSM)),
        ],
        out_specs=[
            pl.BlockSpec((tm, D_SSM), proj_row),
            pl.BlockSpec((tm, D_POOL), proj_row),
            pl.BlockSpec((tm, DT_PAD), proj_row),
            pl.BlockSpec((tm, CONV_DIM), scan_row),
            pl.BlockSpec((tm, D_SSM), scan_row),
        ],
        out_shape=[
            jax.ShapeDtypeStruct((t, D_SSM), F32),
            jax.ShapeDtypeStruct((t, D_POOL), BF16),
            jax.ShapeDtypeStruct((t, DT_PAD), F32),
            jax.ShapeDtypeStruct((t, CONV_DIM), BF16),
            jax.ShapeDtypeStruct((t, D_SSM), F32),
        ],
        scratch_shapes=[
            pltpu.VMEM((RING, tm, CONV_DIM), F32),
            pltpu.VMEM((RING, tm, DT_PAD), F32),
            pltpu.VMEM((GROUPS, D_STATE, GROUP_W), F32),
            pltpu.VMEM((tm, D_MODEL), BF16),
            pltpu.VMEM((tm, D_MODEL), BF16),
            pltpu.VMEM((tm, CONV_DIM), F32),
        ],
        compiler_params=pltpu.CompilerParams(
            dimension_semantics=("arbitrary",), vmem_limit_bytes=VMEM_LIMIT),
        name="proj_ssd_fwd",
    )(x, x, g, wz, wxbc, wdt, wu, cw, cb, dtb, alog, dskip, tril, colsel, headsel)


def _ssd_bwd_kernel(xc_ref, dt_ref, yf_ref, z_ref, u_ref, uprev_ref, unext_ref, x_ref,
                    dtb_ref, alog_ref, triu_ref, colsel_ref, headsel_ref, normw_ref,
                    dmat_ref, poolw_ref, pscale_ref, wout_ref, gpost_ref,
                    o_ref, y_ref, extu_ref, mixin_ring, mix_ref, h_ref, *, tm, nt, ntot):
    j = pl.program_id(0)
    tile = lax.rem(ntot - 1 - jnp.minimum(j, ntot - 1), nt)

    @pl.when(j == 0)
    def _():
        h_ref[...] = jnp.zeros_like(h_ref)
        mixin_ring[...] = jnp.zeros_like(mixin_ring)

    cur = lax.rem(j, 2)
    prv = 1 - cur

    def out_proj(lo):
        mix_ref[:, lo:lo + MXU_N] = _dot(mixin_ring[prv], wout_ref[:, lo:lo + MXU_N])

    pieces = [functools.partial(out_proj, lo) for lo in range(0, D_MODEL, MXU_N)]
    queue = iter(pieces)

    def emit(n):
        for _ in range(n):
            piece = next(queue, None)
            if piece is not None:
                piece()

    nchunk = tm // CHUNK
    plan = iter(BWD_FILL_PLAN * nchunk)

    def fill():
        emit(next(plan, 0))

    a_row = _dt_consts(alog_ref)
    dtb = dtb_ref[...]
    for c in reversed(range(nchunk)):
        r0 = c * CHUNK
        y = _ssd_chunk(xc_ref[r0:r0 + CHUNK, :], dt_ref[r0:r0 + CHUNK, :], dtb, a_row,
                       triu_ref[...], colsel_ref[...], headsel_ref[...], h_ref, reverse=True,
                       fresh=(tile == nt - 1) if c == nchunk - 1 else None, fill=fill)
        y_ref[r0:r0 + CHUNK, :] = y + yf_ref[r0:r0 + CHUNK, :]
    emit(len(pieces))

    v = y_ref[...] * _silu(z_ref[...])
    for g in range(GROUPS):
        lo = g * GROUP_W
        mixin_ring[cur, :, lo:lo + GROUP_W] = (_rms(v[:, lo:lo + GROUP_W])
                                               * normw_ref[:, lo:lo + GROUP_W]).astype(BF16)

    ph = POOL_HALO
    extu_ref[0:ph, :] = jnp.where(tile > 0, uprev_ref[...], jnp.zeros_like(uprev_ref))
    extu_ref[ph:ph + tm, :] = u_ref[...]
    extu_ref[ph + tm:2 * ph + tm, :] = jnp.where(tile < nt - 1, unext_ref[...],
                                                 jnp.zeros_like(unext_ref))
    extu_ref[2 * ph + tm:, :] = jnp.zeros((POOL_K - CHUNK - 2 * ph, D_POOL), BF16)
    for c in range(nchunk):
        r0 = c * CHUNK
        first = jnp.logical_and(tile == 0, c == 0)
        final = jnp.logical_and(tile == nt - 1, c == nchunk - 1)
        variant = jnp.where(first, 0, jnp.where(final, 2, 1))
        for gi in range(len(POOL_WINDOWS)):
            lo = gi * POOL_GROUP_DIM
            band = dmat_ref[variant * len(POOL_WINDOWS) + gi]
            d = _dot(band, extu_ref[r0:r0 + POOL_K, lo:lo + POOL_GROUP_DIM]).astype(BF16)
            pooled = _dot(d, poolw_ref[gi]) * pscale_ref[:, lo:lo + POOL_GROUP_DIM]
            mixin_ring[cur, r0:r0 + CHUNK,
                       D_SSM + lo:D_SSM + lo + POOL_GROUP_DIM] = pooled.astype(BF16)

    o_ref[...] = x_ref[...] + _rms(mix_ref[...]) * gpost_ref[...]


def _ssd_bwd(xc, dt, yf, z, u, x, dtb, alog, triu, colsel, headsel, normw, dmat, poolw, pscale,
             wout, gpost, *, nb, seq, tm):
    nt = seq // tm
    hb = tm // POOL_HALO
    nhalo = nb * seq // POOL_HALO
    ntot = nb * nt
    tile = lambda j: ntot - 1 - jnp.minimum(j, ntot - 1)
    row = lambda j: (tile(j), 0)
    prev = lambda j: (jnp.maximum(tile(j) * hb - 1, 0), 0)
    nxt = lambda j: (jnp.minimum((tile(j) + 1) * hb, nhalo - 1), 0)
    out_row = lambda j: (jnp.minimum(ntot - j, ntot - 1), 0)
    const2 = functools.partial(pl.BlockSpec, index_map=lambda j: (0, 0),
                               pipeline_mode=pl.Buffered(1))
    const3 = functools.partial(pl.BlockSpec, index_map=lambda j: (0, 0, 0),
                               pipeline_mode=pl.Buffered(1))
    t = nb * seq
    nvar = 3 * len(POOL_WINDOWS)
    return pl.pallas_call(
        functools.partial(_ssd_bwd_kernel, tm=tm, nt=nt, ntot=ntot),
        grid=(ntot + 1,),
        in_specs=[
            pl.BlockSpec((tm, CONV_DIM), row),
            pl.BlockSpec((tm, DT_PAD), row),
            pl.BlockSpec((tm, D_SSM), row),
            pl.BlockSpec((tm, D_SSM), row),
            pl.BlockSpec((tm, D_POOL), row),
            pl.BlockSpec((POOL_HALO, D_POOL), prev),
            pl.BlockSpec((POOL_HALO, D_POOL), nxt),
            pl.BlockSpec((tm, D_MODEL), out_row),
            const2((1, DT_PAD)),
            const2((1, DT_PAD)),
            const2((CHUNK, DT_REP * CHUNK)),
            const2((DT_PAD, N_HEADS * CHUNK)),
            const2((DT_PAD, D_SSM)),
            const2((1, D_SSM)),
            const3((nvar, CHUNK, POOL_K)),
            const3((len(POOL_WINDOWS), POOL_GROUP_DIM, POOL_GROUP_DIM)),
            const2((1, D_POOL)),
            const2((D_SSM + D_POOL, D_MODEL)),
            const2((1, D_MODEL)),
        ],
        out_specs=pl.BlockSpec((tm, D_MODEL), out_row),
        out_shape=jax.ShapeDtypeStruct((t, D_MODEL), F32),
        scratch_shapes=[
            pltpu.VMEM((tm, D_SSM), F32),
            pltpu.VMEM((tm + CHUNK, D_POOL), BF16),
            pltpu.VMEM((2, tm, D_SSM + D_POOL), BF16),
            pltpu.VMEM((tm, D_MODEL), F32),
            pltpu.VMEM((GROUPS, D_STATE, GROUP_W), F32),
        ],
        compiler_params=pltpu.CompilerParams(
            dimension_semantics=("arbitrary",), vmem_limit_bytes=VMEM_LIMIT),
        name="ssd_bwd_mix",
    )(xc, dt, yf, z, u, u, u, x, dtb, alog, triu, colsel, headsel, normw, dmat, poolw, pscale,
      wout, gpost)


def _mlp_kernel(x_ref, gpre_ref, w1_ref, w2_ref, gpost_ref, o_ref, h_ref, acc_ref, act_ref):
    f = pl.program_id(1)
    last = pl.num_programs(1) - 1
    blocks = [slice(r, r + MLP_ROWS) for r in range(0, x_ref.shape[0], MLP_ROWS)]

    def sq_relu(a):
        a = jnp.maximum(a, 0.0)
        return (a * a).astype(BF16)

    @pl.when(f == 0)
    def _():
        for rs in blocks:
            h_ref[rs, :] = (_rms(x_ref[rs, :]) * gpre_ref[...]).astype(BF16)
            act_ref[rs, :] = sq_relu(_dot(h_ref[rs, :], w1_ref[...]))
        acc_ref[...] = _dot(act_ref[...], w2_ref[...])

    @pl.when(jnp.logical_and(f > 0, f < last))
    def _():
        acc_ref[...] += _dot(sq_relu(_dot(h_ref[...], w1_ref[...])), w2_ref[...])

    @pl.when(f == last)
    def _():
        act_ref[...] = sq_relu(_dot(h_ref[...], w1_ref[...]))
        for rs in blocks:
            m = acc_ref[rs, :] + _dot(act_ref[rs, :], w2_ref[...])
            o_ref[rs, :] = x_ref[rs, :] + _rms(m) * gpost_ref[...]


def _mlp(x, gpre, w1, w2, gpost, *, tm, tf):
    t = x.shape[0]
    row = lambda i, f: (i, 0)
    fixed = lambda i, f: (0, 0)
    const = functools.partial(pl.BlockSpec, index_map=fixed, pipeline_mode=pl.Buffered(1))
    return pl.pallas_call(
        _mlp_kernel,
        grid=(t // tm, D_FF // tf),
        in_specs=[
            pl.BlockSpec((tm, D_MODEL), row),
            const((1, D_MODEL)),
            pl.BlockSpec((D_MODEL, tf), lambda i, f: (0, f)),
            pl.BlockSpec((tf, D_MODEL), lambda i, f: (f, 0)),
            const((1, D_MODEL)),
        ],
        out_specs=pl.BlockSpec((tm, D_MODEL), row),
        out_shape=jax.ShapeDtypeStruct((t, D_MODEL), F32),
        scratch_shapes=[
            pltpu.VMEM((tm, D_MODEL), BF16),
            pltpu.VMEM((tm, D_MODEL), F32),
            pltpu.VMEM((tm, tf), BF16),
        ],
        compiler_params=pltpu.CompilerParams(
            dimension_semantics=("arbitrary", "arbitrary"), vmem_limit_bytes=VMEM_LIMIT),
        name="mlp",
    )(x, gpre, w1, w2, gpost)


def _selectors():
    q = np.arange(CHUNK)[:, None]
    s = np.arange(CHUNK)[None, :]
    tril = np.tile((s <= q).astype(np.float32), (1, DT_REP))
    triu = np.tile((s >= q).astype(np.float32), (1, DT_REP))
    colsel = np.zeros((2, DT_PAD, N_HEADS * CHUNK), np.float32)
    headsel = np.zeros((2, DT_PAD, D_SSM), np.float32)
    for d in range(2):
        for h in range(N_HEADS):
            for rep in range(DT_REP):
                src = rep * DT_COLS + d * N_HEADS + h
                colsel[d, src, h * CHUNK:(h + 1) * CHUNK] = 1.0
                headsel[d, src, h * HEAD_DIM:(h + 1) * HEAD_DIM] = 1.0
    as_bf16 = lambda a: jnp.asarray(a, dtype=BF16)
    return as_bf16(tril), as_bf16(triu), as_bf16(colsel), as_bf16(headsel)


def _pool_bands(seq):
    out = np.zeros((3, len(POOL_WINDOWS), CHUNK, POOL_K), np.float32)
    for v, start in enumerate((0, CHUNK, seq - CHUNK)):
        t = start + np.arange(CHUNK)[:, None]
        src = start + np.arange(POOL_K)[None, :] - POOL_HALO
        for gi, w in enumerate(POOL_WINDOWS):
            hi = np.minimum(t + w // 2, seq)
            lo = np.maximum(t - w // 2, 0)
            inside = (src >= lo) & (src < hi)
            out[v, gi] = inside / (hi - lo).astype(np.float32) - (src == t)
    return jnp.asarray(out.reshape(3 * len(POOL_WINDOWS), CHUNK, POOL_K), dtype=BF16)


def _layer_params(i, g_mix_pre, g_mix_post, g_mlp_pre, g_mlp_post, w_in, conv_w, conv_b,
                  dt_bias, a_log, d_skip, ssm_norm_w, pool_w, pool_scale, w_out,
                  w_mlp_in, w_mlp_out):
    row = lambda v: v.reshape(1, -1).astype(F32)
    rep_lanes = lambda v: jnp.pad(jnp.tile(v.astype(F32).reshape(-1, DT_COLS), (1, DT_REP)),
                                  ((0, 0), (0, DT_PAD - DT_REP * DT_COLS)))
    o_dt = D_SSM + CONV_DIM
    w = w_in[i]
    return dict(
        g_mix_pre=row(g_mix_pre[i]), g_mix_post=row(g_mix_post[i]),
        g_mlp_pre=row(g_mlp_pre[i]), g_mlp_post=row(g_mlp_post[i]),
        wz=w[:, :D_SSM].astype(BF16),
        wxbc=w[:, D_SSM:o_dt].astype(BF16),
        wdt=rep_lanes(w[:, o_dt:o_dt + DT_COLS]).astype(BF16),
        wu=w[:, o_dt + DT_COLS:].astype(BF16),
        cw=jnp.pad(conv_w[i].astype(F32), ((0, SUBLANES - CONV_K), (0, 0))),
        cb=row(conv_b[i]),
        dtb=rep_lanes(dt_bias[i]), alog=rep_lanes(a_log[i]),
        dskip=row(jnp.repeat(d_skip[i], HEAD_DIM)),
        normw=row(ssm_norm_w[i]),
        poolw=pool_w[i].astype(BF16), pscale=row(pool_scale[i]),
        wout=w_out[i].astype(BF16),
        w1=w_mlp_in[i].astype(BF16), w2=w_mlp_out[i].astype(BF16),
    )


def _trunk(x, layers, consts, *, tm_ssd, tm_mlp):
    nb, seq, _ = x.shape
    tril, triu, colsel, headsel = consts
    dmat = _pool_bands(seq)
    x = x.reshape(nb * seq, D_MODEL)
    for p in layers:
        z, u, dt, xc, yf = _proj_fwd(x, p["g_mix_pre"], p["wz"], p["wxbc"], p["wdt"], p["wu"],
                                     p["cw"], p["cb"], p["dtb"], p["alog"], p["dskip"],
                                     tril, colsel[0], headsel[0], nb=nb, seq=seq, tm=tm_ssd)
        x = _ssd_bwd(xc, dt, yf, z, u, x, p["dtb"], p["alog"], triu, colsel[1], headsel[1],
                     p["normw"], dmat, p["poolw"], p["pscale"], p["wout"], p["g_mix_post"],
                     nb=nb, seq=seq, tm=tm_ssd)
        x = _mlp(x, p["g_mlp_pre"], p["w1"], p["w2"], p["g_mlp_post"], tm=tm_mlp, tf=MLP_TF)
    return x.reshape(nb, seq, D_MODEL)


def kernel(x_prompt, x_sample, g_mix_pre, g_mix_post, g_mlp_pre, g_mlp_post, w_in, conv_w, conv_b, dt_bias, a_log, d_skip, ssm_norm_w, pool_w, pool_scale, w_out, w_mlp_in, w_mlp_out):
    depth = w_in.shape[0]
    layers = [_layer_params(i, g_mix_pre, g_mix_post, g_mlp_pre, g_mlp_post, w_in, conv_w,
                            conv_b, dt_bias, a_log, d_skip, ssm_norm_w, pool_w, pool_scale,
                            w_out, w_mlp_in, w_mlp_out) for i in range(depth)]
    consts = _selectors()
    run = functools.partial(_trunk, layers=layers, consts=consts,
                            tm_ssd=SSD_TM, tm_mlp=MLP_TM)
    return (run(x_prompt), run(x_sample))
```

```python
import functools

import numpy as np
import jax
import jax.numpy as jnp
from jax import lax
from jax.experimental import pallas as pl
from jax.experimental.pallas import tpu as pltpu

D_MODEL = 2048
D_SSM = 1024
HEAD_DIM = 64
N_HEADS = 16
GROUPS = 2
HEADS_PER_GROUP = N_HEADS // GROUPS
GROUP_W = HEADS_PER_GROUP * HEAD_DIM
D_STATE = 128
CONV_K = 5
CHUNK = 128
D_POOL = 1024
POOL_WINDOWS = (2, 4, 8, 16)
POOL_GROUP_DIM = D_POOL // len(POOL_WINDOWS)
CONV_DIM = D_SSM + 2 * GROUPS * D_STATE
D_FF = 4 * D_MODEL
NORM_EPS = 1e-6

LANES = 128
SUBLANES = 8
HALO = SUBLANES
POOL_HALO = 2 * SUBLANES
DT_PAD = LANES
DT_COLS = 2 * N_HEADS
DT_REP = 3
HEADS_PER_DOT = 4
POOL_K = 2 * CHUNK
MXU_N = 256
SSD_TM = 2 * CHUNK
MLP_TM = 1024
MLP_TF = 512
MLP_ROWS = 256
FWD_LAG = 2
FWD_FILL_CONV = 1
FWD_FILL_PLAN = (1, 1, 1, 1, 1, 1)
BWD_FILL_PLAN = (1, 1, 1, 1, 1, 1)
RING = 4
NEG_BIG = -1e30
VMEM_LIMIT = 56 * 1024 * 1024

F32 = jnp.float32
BF16 = jnp.bfloat16


def _dot(a, b):
    return jnp.dot(a, b, preferred_element_type=F32)


def _dot_nt(a, b):
    return lax.dot_general(a, b, (((1,), (1,)), ((), ())), preferred_element_type=F32)


def _split3(v):
    hi = v.astype(BF16)
    r1 = v - hi.astype(F32)
    lo = r1.astype(BF16)
    lo2 = (r1 - lo.astype(F32)).astype(BF16)
    return hi, lo, lo2


def _lane_split(v):
    hi, lo, lo2 = _split3(v)
    lane = lax.broadcasted_iota(jnp.int32, v.shape, 1)
    return jnp.where(lane < DT_COLS, hi, jnp.where(lane < 2 * DT_COLS, lo, lo2))


def _exact_rdot(v, sel):
    return _dot(_lane_split(v), sel)


def _exact_cumsum(sel3, v):
    return _dot(sel3, jnp.concatenate(_split3(v), axis=0))


def _silu(v):
    return v * (1.0 / (1.0 + jnp.exp(-v)))


def _softplus(v):
    return jnp.maximum(v, 0.0) + jnp.log1p(jnp.exp(-jnp.abs(v)))


def _rms(v):
    return v * lax.rsqrt(jnp.mean(v * v, axis=-1, keepdims=True) + NORM_EPS)


def _ssd_chunk(xc, dtr, dtb, a_row, cum_sel, colsel, headsel, h_ref, *, reverse, fresh=None,
               fill=lambda: None):
    lane_off = N_HEADS if reverse else 0
    edge = 0 if reverse else CHUNK - 1
    xs = xc[:, :D_SSM].astype(F32)
    xs_b = xc[:, :D_SSM].astype(BF16)

    dt = _softplus(dtr + dtb)
    acum = _exact_cumsum(cum_sel, dt * a_row)
    fill()
    g_t = (acum - jnp.log(dt)).T
    a_col = _exact_rdot(acum, colsel)
    a_edge = acum[edge:edge + 1, :]
    w_exp = _exact_rdot(jnp.exp(a_edge - acum) * dt, headsel)
    e_exp = _exact_rdot(jnp.exp(acum), headsel)
    fill()
    xw_b = (w_exp * xs).astype(BF16)

    q_idx = lax.broadcasted_iota(jnp.int32, (CHUNK, CHUNK), 0)
    s_idx = lax.broadcasted_iota(jnp.int32, (CHUNK, CHUNK), 1)
    allowed = (s_idx >= q_idx) if reverse else (s_idx <= q_idx)
    quad_w = HEADS_PER_DOT * HEAD_DIM
    head_of_lane = lax.broadcasted_iota(jnp.int32, (CHUNK, quad_w), 1) // HEAD_DIM

    pieces = []
    for g in range(GROUPS):
        b_g = xc[:, D_SSM + g * D_STATE:D_SSM + (g + 1) * D_STATE]
        c_g = xc[:, D_SSM + GROUPS * D_STATE + g * D_STATE:
                 D_SSM + GROUPS * D_STATE + (g + 1) * D_STATE].astype(BF16)
        scores = _dot_nt(c_g, b_g.astype(BF16))
        h_prev = h_ref[g]
        if fresh is not None:
            h_prev = jnp.where(fresh, 0.0, h_prev)
        y_off = _dot(c_g, h_prev.astype(BF16))
        for qd in range(HEADS_PER_GROUP // HEADS_PER_DOT):
            h0 = g * HEADS_PER_GROUP + HEADS_PER_DOT * qd
            lo = h0 * HEAD_DIM
            wmats = []
            for hh in range(h0, h0 + HEADS_PER_DOT):
                seg = (a_col[:, hh * CHUNK:(hh + 1) * CHUNK]
                       - g_t[lane_off + hh:lane_off + hh + 1, :])
                wmats.append((scores * jnp.exp(jnp.where(allowed, seg, NEG_BIG))).astype(BF16))
            x_quad = xs_b[:, lo:lo + quad_w]
            x_diag = jnp.concatenate(
                [jnp.where(head_of_lane == k, x_quad, jnp.zeros_like(x_quad))
                 for k in range(HEADS_PER_DOT)], axis=0)
            y_diag = _dot(jnp.concatenate(wmats, axis=1), x_diag)
            fill()
            pieces.append(y_diag + y_off[:, qd * quad_w:(qd + 1) * quad_w]
                          * e_exp[:, lo:lo + quad_w])
        states = _dot(b_g.astype(F32).T.astype(BF16), xw_b[:, g * GROUP_W:(g + 1) * GROUP_W])
        decay = e_exp[edge:edge + 1, g * GROUP_W:(g + 1) * GROUP_W]
        h_ref[g] = decay * h_prev + states
    return jnp.concatenate(pieces, axis=-1)


def _dt_consts(alog_ref):
    lane = lax.broadcasted_iota(jnp.int32, (1, DT_PAD), 1)
    return jnp.where(lane < DT_REP * DT_COLS, -jnp.exp(alog_ref[...]), 0.0)


def _proj_fwd_kernel(x0_ref, xn_ref, g_ref, wz_ref, wxbc_ref, wdt_ref, wu_ref, cw_ref, cb_ref,
                     dtb_ref, alog_ref, dskip_ref, tril_ref, colsel_ref, headsel_ref,
                     z_ref, u_ref, dt_ref, xc_ref, yf_ref,
                     xbc_ring, dt_ring, h_ref, hn_ref, hn_next_ref, xcf_ref, *, tm, nt):
    i = pl.program_id(0)

    @pl.when(i == 0)
    def _():
        xbc_ring[...] = jnp.zeros_like(xbc_ring)
        dt_ring[...] = jnp.zeros_like(dt_ring)
        h_ref[...] = jnp.zeros_like(h_ref)
        hn_next_ref[...] = (_rms(x0_ref[...]) * g_ref[...]).astype(BF16)

    t = lax.rem(i - FWD_LAG + nt, nt)
    cur = lax.rem(i + RING - FWD_LAG, RING)
    prv = lax.rem(i + RING - FWD_LAG - 1, RING)
    nxt = lax.rem(i + RING - FWD_LAG + 1, RING)
    head = jnp.where(t > 0, xbc_ring[prv, tm - HALO:tm, :], 0.0)
    tail = jnp.where(t < nt - 1, xbc_ring[nxt, 0:HALO, :], 0.0)
    nchunk = tm // CHUNK
    mid = CONV_K // 2
    rows = CHUNK + 2 * HALO
    slot = lax.rem(i, RING)
    a_row = _dt_consts(alog_ref)
    dtb = dtb_ref[...]
    dts = dt_ring[cur]
    hn_ref[...] = hn_next_ref[...]

    def conv(c):
        r0 = c * CHUNK
        blk = jnp.concatenate([
            head if c == 0 else xbc_ring[cur, r0 - HALO:r0, :],
            xbc_ring[cur, r0:r0 + CHUNK, :],
            tail if c == nchunk - 1 else xbc_ring[cur, r0 + CHUNK:r0 + CHUNK + HALO, :],
        ], axis=0)
        acc = cb_ref[...] + cw_ref[mid:mid + 1, :] * blk[HALO:HALO + CHUNK, :]
        for k in range(CONV_K):
            if k != mid:
                rolled = pltpu.roll(blk, rows - (HALO - mid + k), axis=0)
                acc = acc + cw_ref[k:k + 1, :] * rolled[0:CHUNK, :]
        xc = _silu(acc)
        xcf_ref[r0:r0 + CHUNK, :] = xc
        xc_ref[r0:r0 + CHUNK, :] = xc.astype(BF16)

    def scan(c):
        r0 = c * CHUNK
        xc = xcf_ref[r0:r0 + CHUNK, :]
        y = _ssd_chunk(xc, dts[r0:r0 + CHUNK, :], dtb, a_row, tril_ref[...], colsel_ref[...],
                       headsel_ref[...], h_ref, reverse=False,
                       fresh=(t == 0) if c == 0 else None, fill=fill)
        yf_ref[r0:r0 + CHUNK, :] = y + dskip_ref[...] * xc[:, :D_SSM]

    def proj_z(lo):
        z_ref[:, lo:lo + MXU_N] = _dot(hn_ref[...], wz_ref[:, lo:lo + MXU_N])

    def proj_u(lo):
        u_ref[:, lo:lo + MXU_N] = _dot(hn_ref[...], wu_ref[:, lo:lo + MXU_N]).astype(BF16)

    def proj_dt():
        dtr = _dot(hn_ref[...], wdt_ref[...])
        dt_ref[...] = dtr
        dt_ring[slot] = dtr

    def proj_xbc(lo):
        xbc_ring[slot, :, lo:lo + MXU_N] = _dot(hn_ref[...], wxbc_ref[:, lo:lo + MXU_N])

    pieces = ([functools.partial(proj_z, lo) for lo in range(0, D_SSM, MXU_N)]
              + [functools.partial(proj_u, lo) for lo in range(0, D_POOL, MXU_N)]
              + [proj_dt]
              + [functools.partial(proj_xbc, lo) for lo in range(0, CONV_DIM, MXU_N)])
    queue = iter(pieces)

    def emit(n):
        for _ in range(n):
            piece = next(queue, None)
            if piece is not None:
                piece()

    plan = iter(FWD_FILL_PLAN * nchunk)

    def fill():
        emit(next(plan, 0))

    for c in range(nchunk):
        emit(FWD_FILL_CONV)
        conv(c)
    for c in range(nchunk):
        scan(c)
    emit(len(pieces))
    hn_next_ref[...] = (_rms(xn_ref[...]) * g_ref[...]).astype(BF16)


def _proj_fwd(x, g, wz, wxbc, wdt, wu, cw, cb, dtb, alog, dskip, tril, colsel, headsel,
              *, nb, seq, tm):
    nt = seq // tm
    ntot = nb * nt
    proj_row = lambda i: (jnp.minimum(i, ntot - 1), 0)
    norm_row = lambda i: (jnp.minimum(i + 1, ntot - 1), 0)
    scan_row = lambda i: (jnp.clip(i - FWD_LAG, 0, ntot - 1), 0)
    const = functools.partial(pl.BlockSpec, index_map=lambda i: (0, 0),
                              pipeline_mode=pl.Buffered(1))
    t = nb * seq
    return pl.pallas_call(
        functools.partial(_proj_fwd_kernel, tm=tm, nt=nt),
        grid=(ntot + FWD_LAG,),
        in_specs=[
            pl.BlockSpec((tm, D_MODEL), lambda i: (0, 0), pipeline_mode=pl.Buffered(1)),
            pl.BlockSpec((tm, D_MODEL), norm_row),
            const((1, D_MODEL)),
            const((D_MODEL, D_SSM)),
            const((D_MODEL, CONV_DIM)),
            const((D_MODEL, DT_PAD)),
            const((D_MODEL, D_POOL)),
            const((SUBLANES, CONV_DIM)),
            const((1, CONV_DIM)),
            const((1, DT_PAD)),
            const((1, DT_PAD)),
            const((1, D_SSM)),
            const((CHUNK, DT_REP * CHUNK)),
            const((DT_PAD, N_HEADS * CHUNK)),
            const((DT_PAD, D_SSM)),
        ],
        out_specs=[
            pl.BlockSpec((tm, D_SSM), proj_row),
            pl.BlockSpec((tm, D_POOL), proj_row),
            pl.BlockSpec((tm, DT_PAD), proj_row),
            pl.BlockSpec((tm, CONV_DIM), scan_row),
            pl.BlockSpec((tm, D_SSM), scan_row),
        ],
        out_shape=[
            jax.ShapeDtypeStruct((t, D_SSM), F32),
            jax.ShapeDtypeStruct((t, D_POOL), BF16),
            jax.ShapeDtypeStruct((t, DT_PAD), F32),
            jax.ShapeDtypeStruct((t, CONV_DIM), BF16),
            jax.ShapeDtypeStruct((t, D_SSM), F32),
        ],
        scratch_shapes=[
            pltpu.VMEM((RING, tm, CONV_DIM), F32),
            pltpu.VMEM((RING, tm, DT_PAD), F32),
            pltpu.VMEM((GROUPS, D_STATE, GROUP_W), F32),
            pltpu.VMEM((tm, D_MODEL), BF16),
            pltpu.VMEM((tm, D_MODEL), BF16),
            pltpu.VMEM((tm, CONV_DIM), F32),
        ],
        compiler_params=pltpu.CompilerParams(
            dimension_semantics=("arbitrary",), vmem_limit_bytes=VMEM_LIMIT),
        name="proj_ssd_fwd",
    )(x, x, g, wz, wxbc, wdt, wu, cw, cb, dtb, alog, dskip, tril, colsel, headsel)


def _ssd_bwd_kernel(xc_ref, dt_ref, yf_ref, z_ref, u_ref, uprev_ref, unext_ref, x_ref,
                    dtb_ref, alog_ref, triu_ref, colsel_ref, headsel_ref, normw_ref,
                    dmat_ref, poolw_ref, pscale_ref, wout_ref, gpost_ref,
                    o_ref, y_ref, extu_ref, mixin_ring, mix_ref, h_ref, *, tm, nt, ntot):
    j = pl.program_id(0)
    tile = lax.rem(ntot - 1 - jnp.minimum(j, ntot - 1), nt)

    @pl.when(j == 0)
    def _():
        h_ref[...] = jnp.zeros_like(h_ref)
        mixin_ring[...] = jnp.zeros_like(mixin_ring)

    cur = lax.rem(j, 2)
    prv = 1 - cur

    def out_proj(lo):
        mix_ref[:, lo:lo + MXU_N] = _dot(mixin_ring[prv], wout_ref[:, lo:lo + MXU_N])

    pieces = [functools.partial(out_proj, lo) for lo in range(0, D_MODEL, MXU_N)]
    queue = iter(pieces)

    def emit(n):
        for _ in range(n):
            piece = next(queue, None)
            if piece is not None:
                piece()

    nchunk = tm // CHUNK
    plan = iter(BWD_FILL_PLAN * nchunk)

    def fill():
        emit(next(plan, 0))

    a_row = _dt_consts(alog_ref)
    dtb = dtb_ref[...]
    for c in reversed(range(nchunk)):
        r0 = c * CHUNK
        y = _ssd_chunk(xc_ref[r0:r0 + CHUNK, :], dt_ref[r0:r0 + CHUNK, :], dtb, a_row,
                       triu_ref[...], colsel_ref[...], headsel_ref[...], h_ref, reverse=True,
                       fresh=(tile == nt - 1) if c == nchunk - 1 else None, fill=fill)
        y_ref[r0:r0 + CHUNK, :] = y + yf_ref[r0:r0 + CHUNK, :]
    emit(len(pieces))

    v = y_ref[...] * _silu(z_ref[...])
    for g in range(GROUPS):
        lo = g * GROUP_W
        mixin_ring[cur, :, lo:lo + GROUP_W] = (_rms(v[:, lo:lo + GROUP_W])
                                               * normw_ref[:, lo:lo + GROUP_W]).astype(BF16)

    ph = POOL_HALO
    extu_ref[0:ph, :] = jnp.where(tile > 0, uprev_ref[...], jnp.zeros_like(uprev_ref))
    extu_ref[ph:ph + tm, :] = u_ref[...]
    extu_ref[ph + tm:2 * ph + tm, :] = jnp.where(tile < nt - 1, unext_ref[...],
                                                 jnp.zeros_like(unext_ref))
    extu_ref[2 * ph + tm:, :] = jnp.zeros((POOL_K - CHUNK - 2 * ph, D_POOL), BF16)
    for c in range(nchunk):
        r0 = c * CHUNK
        first = jnp.logical_and(tile == 0, c == 0)
        final = jnp.logical_and(tile == nt - 1, c == nchunk - 1)
        variant = jnp.where(first, 0, jnp.where(final, 2, 1))
        for gi in range(len(POOL_WINDOWS)):
            lo = gi * POOL_GROUP_DIM
            band = dmat_ref[variant * len(POOL_WINDOWS) + gi]
            d = _dot(band, extu_ref[r0:r0 + POOL_K, lo:lo + POOL_GROUP_DIM]).astype(BF16)
            pooled = _dot(d, poolw_ref[gi]) * pscale_ref[:, lo:lo + POOL_GROUP_DIM]
            mixin_ring[cur, r0:r0 + CHUNK,
                       D_SSM + lo:D_SSM + lo + POOL_GROUP_DIM] = pooled.astype(BF16)

    o_ref[...] = x_ref[...] + _rms(mix_ref[...]) * gpost_ref[...]


def _ssd_bwd(xc, dt, yf, z, u, x, dtb, alog, triu, colsel, headsel, normw, dmat, poolw, pscale,
             wout, gpost, *, nb, seq, tm):
    nt = seq // tm
    hb = tm // POOL_HALO
    nhalo = nb * seq // POOL_HALO
    ntot = nb * nt
    tile = lambda j: ntot - 1 - jnp.minimum(j, ntot - 1)
    row = lambda j: (tile(j), 0)
    prev = lambda j: (jnp.maximum(tile(j) * hb - 1, 0), 0)
    nxt = lambda j: (jnp.minimum((tile(j) + 1) * hb, nhalo - 1), 0)
    out_row = lambda j: (jnp.minimum(ntot - j, ntot - 1), 0)
    const2 = functools.partial(pl.BlockSpec, index_map=lambda j: (0, 0),
                               pipeline_mode=pl.Buffered(1))
    const3 = functools.partial(pl.BlockSpec, index_map=lambda j: (0, 0, 0),
                               pipeline_mode=pl.Buffered(1))
    t = nb * seq
    nvar = 3 * len(POOL_WINDOWS)
    return pl.pallas_call(
        functools.partial(_ssd_bwd_kernel, tm=tm, nt=nt, ntot=ntot),
        grid=(ntot + 1,),
        in_specs=[
            pl.BlockSpec((tm, CONV_DIM), row),
            pl.BlockSpec((tm, DT_PAD), row),
            pl.BlockSpec((tm, D_SSM), row),
            pl.BlockSpec((tm, D_SSM), row),
            pl.BlockSpec((tm, D_POOL), row),
            pl.BlockSpec((POOL_HALO, D_POOL), prev),
            pl.BlockSpec((POOL_HALO, D_POOL), nxt),
            pl.BlockSpec((tm, D_MODEL), out_row),
            const2((1, DT_PAD)),
            const2((1, DT_PAD)),
            const2((CHUNK, DT_REP * CHUNK)),
            const2((DT_PAD, N_HEADS * CHUNK)),
            const2((DT_PAD, D_SSM)),
            const2((1, D_SSM)),
            const3((nvar, CHUNK, POOL_K)),
            const3((len(POOL_WINDOWS), POOL_GROUP_DIM, POOL_GROUP_DIM)),
            const2((1, D_POOL)),
            const2((D_SSM + D_POOL, D_MODEL)),
            const2((1, D_MODEL)),
        ],
        out_specs=pl.BlockSpec((tm, D_MODEL), out_row),
        out_shape=jax.ShapeDtypeStruct((t, D_MODEL), F32),
        scratch_shapes=[
            pltpu.VMEM((tm, D_SSM), F32),
            pltpu.VMEM((tm + CHUNK, D_POOL), BF16),
            pltpu.VMEM((2, tm, D_SSM + D_POOL), BF16),
            pltpu.VMEM((tm, D_MODEL), F32),
            pltpu.VMEM((GROUPS, D_STATE, GROUP_W), F32),
        ],
        compiler_params=pltpu.CompilerParams(
            dimension_semantics=("arbitrary",), vmem_limit_bytes=VMEM_LIMIT),
        name="ssd_bwd_mix",
    )(xc, dt, yf, z, u, u, u, x, dtb, alog, triu, colsel, headsel, normw, dmat, poolw, pscale,
      wout, gpost)


def _mlp_kernel(x_ref, gpre_ref, w1_ref, w2_ref, gpost_ref, o_ref, h_ref, acc_ref, act_ref):
    f = pl.program_id(1)
    last = pl.num_programs(1) - 1
    blocks = [slice(r, r + MLP_ROWS) for r in range(0, x_ref.shape[0], MLP_ROWS)]

    def sq_relu(a):
        a = jnp.maximum(a, 0.0)
        return (a * a).astype(BF16)

    @pl.when(f == 0)
    def _():
        for rs in blocks:
            h_ref[rs, :] = (_rms(x_ref[rs, :]) * gpre_ref[...]).astype(BF16)
            act_ref[rs, :] = sq_relu(_dot(h_ref[rs, :], w1_ref[...]))
        acc_ref[...] = _dot(act_ref[...], w2_ref[...])

    @pl.when(jnp.logical_and(f > 0, f < last))
    def _():
        acc_ref[...] += _dot(sq_relu(_dot(h_ref[...], w1_ref[...])), w2_ref[...])

    @pl.when(f == last)
    def _():
        act_ref[...] = sq_relu(_dot(h_ref[...], w1_ref[...]))
        for rs in blocks:
            m = acc_ref[rs, :] + _dot(act_ref[rs, :], w2_ref[...])
            o_ref[rs, :] = x_ref[rs, :] + _rms(m) * gpost_ref[...]


def _mlp(x, gpre, w1, w2, gpost, *, tm, tf):
    t = x.shape[0]
    row = lambda i, f: (i, 0)
    fixed = lambda i, f: (0, 0)
    const = functools.partial(pl.BlockSpec, index_map=fixed, pipeline_mode=pl.Buffered(1))
    return pl.pallas_call(
        _mlp_kernel,
        grid=(t // tm, D_FF // tf),
        in_specs=[
            pl.BlockSpec((tm, D_MODEL), row),
            const((1, D_MODEL)),
            pl.BlockSpec((D_MODEL, tf), lambda i, f: (0, f)),
            pl.BlockSpec((tf, D_MODEL), lambda i, f: (f, 0)),
            const((1, D_MODEL)),
        ],
        out_specs=pl.BlockSpec((tm, D_MODEL), row),
        out_shape=jax.ShapeDtypeStruct((t, D_MODEL), F32),
        scratch_shapes=[
            pltpu.VMEM((tm, D_MODEL), BF16),
            pltpu.VMEM((tm, D_MODEL), F32),
            pltpu.VMEM((tm, tf), BF16),
        ],
        compiler_params=pltpu.CompilerParams(
            dimension_semantics=("arbitrary", "arbitrary"), vmem_limit_bytes=VMEM_LIMIT),
        name="mlp",
    )(x, gpre, w1, w2, gpost)


def _selectors():
    q = np.arange(CHUNK)[:, None]
    s = np.arange(CHUNK)[None, :]
    tril = np.tile((s <= q).astype(np.float32), (1, DT_REP))
    triu = np.tile((s >= q).astype(np.float32), (1, DT_REP))
    colsel = np.zeros((2, DT_PAD, N_HEADS * CHUNK), np.float32)
    headsel = np.zeros((2, DT_PAD, D_SSM), np.float32)
    for d in range(2):
        for h in range(N_HEADS):
            for rep in range(DT_REP):
                src = rep * DT_COLS + d * N_HEADS + h
                colsel[d, src, h * CHUNK:(h + 1) * CHUNK] = 1.0
                headsel[d, src, h * HEAD_DIM:(h + 1) * HEAD_DIM] = 1.0
    as_bf16 = lambda a: jnp.asarray(a, dtype=BF16)
    return as_bf16(tril), as_bf16(triu), as_bf16(colsel), as_bf16(headsel)


def _pool_bands(seq):
    out = np.zeros((3, len(POOL_WINDOWS), CHUNK, POOL_K), np.float32)
    for v, start in enumerate((0, CHUNK, seq - CHUNK)):
        t = start + np.arange(CHUNK)[:, None]
        src = start + np.arange(POOL_K)[None, :] - POOL_HALO
        for gi, w in enumerate(POOL_WINDOWS):
            hi = np.minimum(t + w // 2, seq)
            lo = np.maximum(t - w // 2, 0)
            inside = (src >= lo) & (src < hi)
            out[v, gi] = inside / (hi - lo).astype(np.float32) - (src == t)
    return jnp.asarray(out.reshape(3 * len(POOL_WINDOWS), CHUNK, POOL_K), dtype=BF16)


def _layer_params(i, g_mix_pre, g_mix_post, g_mlp_pre, g_mlp_post, w_in, conv_w, conv_b,
                  dt_bias, a_log, d_skip, ssm_norm_w, pool_w, pool_scale, w_out,
                  w_mlp_in, w_mlp_out):
    row = lambda v: v.reshape(1, -1).astype(F32)
    rep_lanes = lambda v: jnp.pad(jnp.tile(v.astype(F32).reshape(-1, DT_COLS), (1, DT_REP)),
                                  ((0, 0), (0, DT_PAD - DT_REP * DT_COLS)))
    o_dt = D_SSM + CONV_DIM
    w = w_in[i]
    return dict(
        g_mix_pre=row(g_mix_pre[i]), g_mix_post=row(g_mix_post[i]),
        g_mlp_pre=row(g_mlp_pre[i]), g_mlp_post=row(g_mlp_post[i]),
        wz=w[:, :D_SSM].astype(BF16),
        wxbc=w[:, D_SSM:o_dt].astype(BF16),
        wdt=rep_lanes(w[:, o_dt:o_dt + DT_COLS]).astype(BF16),
        wu=w[:, o_dt + DT_COLS:].astype(BF16),
        cw=jnp.pad(conv_w[i].astype(F32), ((0, SUBLANES - CONV_K), (0, 0))),
        cb=row(conv_b[i]),
        dtb=rep_lanes(dt_bias[i]), alog=rep_lanes(a_log[i]),
        dskip=row(jnp.repeat(d_skip[i], HEAD_DIM)),
        normw=row(ssm_norm_w[i]),
        poolw=pool_w[i].astype(BF16), pscale=row(pool_scale[i]),
        wout=w_out[i].astype(BF16),
        w1=w_mlp_in[i].astype(BF16), w2=w_mlp_out[i].astype(BF16),
    )


def _trunk(x, layers, consts, *, tm_ssd, tm_mlp):
    nb, seq, _ = x.shape
    tril, triu, colsel, headsel = consts
    dmat = _pool_bands(seq)
    x = x.reshape(nb * seq, D_MODEL)
    for p in layers:
        z, u, dt, xc, yf = _proj_fwd(x, p["g_mix_pre"], p["wz"], p["wxbc"], p["wdt"], p["wu"],
                                     p["cw"], p["cb"], p["dtb"], p["alog"], p["dskip"],
                                     tril, colsel[0], headsel[0], nb=nb, seq=seq, tm=tm_ssd)
        x = _ssd_bwd(xc, dt, yf, z, u, x, p["dtb"], p["alog"], triu, colsel[1], headsel[1],
                     p["normw"], dmat, p["poolw"], p["pscale"], p["wout"], p["g_mix_post"],
                     nb=nb, seq=seq, tm=tm_ssd)
        x = _mlp(x, p["g_mlp_pre"], p["w1"], p["w2"], p["g_mlp_post"], tm=tm_mlp, tf=MLP_TF)
    return x.reshape(nb, seq, D_MODEL)


def kernel(x_prompt, x_sample, g_mix_pre, g_mix_post, g_mlp_pre, g_mlp_post, w_in, conv_w, conv_b, dt_bias, a_log, d_skip, ssm_norm_w, pool_w, pool_scale, w_out, w_mlp_in, w_mlp_out):
    depth = w_in.shape[0]
    layers = [_layer_params(i, g_mix_pre, g_mix_post, g_mlp_pre, g_mlp_post, w_in, conv_w,
                            conv_b, dt_bias, a_log, d_skip, ssm_norm_w, pool_w, pool_scale,
                            w_out, w_mlp_in, w_mlp_out) for i in range(depth)]
    consts = _selectors()
    run = functools.partial(_trunk, layers=layers, consts=consts,
                            tm_ssd=SSD_TM, tm_mlp=MLP_TM)
    return (run(x_prompt), run(x_sample))
```

```python
import functools

import numpy as np
import jax
import jax.numpy as jnp
from jax import lax
from jax.experimental import pallas as pl
from jax.experimental.pallas import tpu as pltpu

D_MODEL = 2048
D_SSM = 1024
HEAD_DIM = 64
N_HEADS = 16
GROUPS = 2
HEADS_PER_GROUP = N_HEADS // GROUPS
GROUP_W = HEADS_PER_GROUP * HEAD_DIM
D_STATE = 128
CONV_K = 5
CHUNK = 128
D_POOL = 1024
POOL_WINDOWS = (2, 4, 8, 16)
POOL_GROUP_DIM = D_POOL // len(POOL_WINDOWS)
CONV_DIM = D_SSM + 2 * GROUPS * D_STATE
D_FF = 4 * D_MODEL
NORM_EPS = 1e-6

LANES = 128
SUBLANES = 8
HALO = SUBLANES
POOL_HALO = 2 * SUBLANES
DT_PAD = LANES
DT_COLS = 2 * N_HEADS
DT_REP = 3
HEADS_PER_DOT = 4
POOL_K = 2 * CHUNK
MXU_N = 256
SSD_TM = 2 * CHUNK
MLP_TM = 512
MLP_TF = 1024
MLP_ROWS = 256
FWD_LAG = 2
FWD_FILL_CONV = 1
FWD_FILL_PLAN = (1, 1, 1, 1, 1, 1)
BWD_FILL_PLAN = (1, 1, 1, 1, 1, 1)
RING = 4
NEG_BIG = -1e30
VMEM_LIMIT = 56 * 1024 * 1024

F32 = jnp.float32
BF16 = jnp.bfloat16


def _dot(a, b):
    return jnp.dot(a, b, preferred_element_type=F32)


def _dot_nt(a, b):
    return lax.dot_general(a, b, (((1,), (1,)), ((), ())), preferred_element_type=F32)


def _split3(v):
    hi = v.astype(BF16)
    r1 = v - hi.astype(F32)
    lo = r1.astype(BF16)
    lo2 = (r1 - lo.astype(F32)).astype(BF16)
    return hi, lo, lo2


def _lane_split(v):
    hi, lo, lo2 = _split3(v)
    lane = lax.broadcasted_iota(jnp.int32, v.shape, 1)
    return jnp.where(lane < DT_COLS, hi, jnp.where(lane < 2 * DT_COLS, lo, lo2))


def _exact_rdot(v, sel):
    return _dot(_lane_split(v), sel)


def _exact_cumsum(sel3, v):
    return _dot(sel3, jnp.concatenate(_split3(v), axis=0))


def _silu(v):
    return v * (1.0 / (1.0 + jnp.exp(-v)))


def _softplus(v):
    return jnp.maximum(v, 0.0) + jnp.log1p(jnp.exp(-jnp.abs(v)))


def _rms(v):
    return v * lax.rsqrt(jnp.mean(v * v, axis=-1, keepdims=True) + NORM_EPS)


def _ssd_chunk(xc, dtr, dtb, a_row, cum_sel, colsel, headsel, h_ref, *, reverse, fresh=None,
               fill=lambda: None):
    lane_off = N_HEADS if reverse else 0
    edge = 0 if reverse else CHUNK - 1
    xs = xc[:, :D_SSM].astype(F32)
    xs_b = xc[:, :D_SSM].astype(BF16)

    dt = _softplus(dtr + dtb)
    acum = _exact_cumsum(cum_sel, dt * a_row)
    fill()
    g_t = (acum - jnp.log(dt)).T
    a_col = _exact_rdot(acum, colsel)
    a_edge = acum[edge:edge + 1, :]
    w_exp = _exact_rdot(jnp.exp(a_edge - acum) * dt, headsel)
    e_exp = _exact_rdot(jnp.exp(acum), headsel)
    fill()
    xw_b = (w_exp * xs).astype(BF16)

    q_idx = lax.broadcasted_iota(jnp.int32, (CHUNK, CHUNK), 0)
    s_idx = lax.broadcasted_iota(jnp.int32, (CHUNK, CHUNK), 1)
    allowed = (s_idx >= q_idx) if reverse else (s_idx <= q_idx)
    quad_w = HEADS_PER_DOT * HEAD_DIM
    head_of_lane = lax.broadcasted_iota(jnp.int32, (CHUNK, quad_w), 1) // HEAD_DIM

    pieces = []
    for g in range(GROUPS):
        b_g = xc[:, D_SSM + g * D_STATE:D_SSM + (g + 1) * D_STATE]
        c_g = xc[:, D_SSM + GROUPS * D_STATE + g * D_STATE:
                 D_SSM + GROUPS * D_STATE + (g + 1) * D_STATE].astype(BF16)
        scores = _dot_nt(c_g, b_g.astype(BF16))
        h_prev = h_ref[g]
        if fresh is not None:
            h_prev = jnp.where(fresh, 0.0, h_prev)
        y_off = _dot(c_g, h_prev.astype(BF16))
        for qd in range(HEADS_PER_GROUP // HEADS_PER_DOT):
            h0 = g * HEADS_PER_GROUP + HEADS_PER_DOT * qd
            lo = h0 * HEAD_DIM
            wmats = []
            for hh in range(h0, h0 + HEADS_PER_DOT):
                seg = (a_col[:, hh * CHUNK:(hh + 1) * CHUNK]
                       - g_t[lane_off + hh:lane_off + hh + 1, :])
                wmats.append((scores * jnp.exp(jnp.where(allowed, seg, NEG_BIG))).astype(BF16))
            x_quad = xs_b[:, lo:lo + quad_w]
            x_diag = jnp.concatenate(
                [jnp.where(head_of_lane == k, x_quad, jnp.zeros_like(x_quad))
                 for k in range(HEADS_PER_DOT)], axis=0)
            y_diag = _dot(jnp.concatenate(wmats, axis=1), x_diag)
            fill()
            pieces.append(y_diag + y_off[:, qd * quad_w:(qd + 1) * quad_w]
                          * e_exp[:, lo:lo + quad_w])
        states = _dot(b_g.astype(F32).T.astype(BF16), xw_b[:, g * GROUP_W:(g + 1) * GROUP_W])
        decay = e_exp[edge:edge + 1, g * GROUP_W:(g + 1) * GROUP_W]
        h_ref[g] = decay * h_prev + states
    return jnp.concatenate(pieces, axis=-1)


def _dt_consts(alog_ref):
    lane = lax.broadcasted_iota(jnp.int32, (1, DT_PAD), 1)
    return jnp.where(lane < DT_REP * DT_COLS, -jnp.exp(alog_ref[...]), 0.0)


def _proj_fwd_kernel(x0_ref, xn_ref, g_ref, wzx_ref, wdt_ref, wu_ref, cw_ref, cb_ref,
                     dtb_ref, alog_ref, dskip_ref, tril_ref, colsel_ref, headsel_ref,
                     z_ref, u_ref, dt_ref, xc_ref, yf_ref,
                     xbc_ring, dt_ring, h_ref, hn_ref, hn_next_ref, xcf_ref, *, tm, nt):
    i = pl.program_id(0)

    @pl.when(i == 0)
    def _():
        xbc_ring[...] = jnp.zeros_like(xbc_ring)
        dt_ring[...] = jnp.zeros_like(dt_ring)
        h_ref[...] = jnp.zeros_like(h_ref)
        hn_next_ref[...] = (_rms(x0_ref[...]) * g_ref[...]).astype(BF16)

    t = lax.rem(i - FWD_LAG + nt, nt)
    cur = lax.rem(i + RING - FWD_LAG, RING)
    prv = lax.rem(i + RING - FWD_LAG - 1, RING)
    nxt = lax.rem(i + RING - FWD_LAG + 1, RING)
    head = jnp.where(t > 0, xbc_ring[prv, tm - HALO:tm, :], 0.0)
    tail = jnp.where(t < nt - 1, xbc_ring[nxt, 0:HALO, :], 0.0)
    nchunk = tm // CHUNK
    mid = CONV_K // 2
    rows = CHUNK + 2 * HALO
    slot = lax.rem(i, RING)
    a_row = _dt_consts(alog_ref)
    dtb = dtb_ref[...]
    dts = dt_ring[cur]
    hn_ref[...] = hn_next_ref[...]

    def conv(c):
        r0 = c * CHUNK
        blk = jnp.concatenate([
            head if c == 0 else xbc_ring[cur, r0 - HALO:r0, :],
            xbc_ring[cur, r0:r0 + CHUNK, :],
            tail if c == nchunk - 1 else xbc_ring[cur, r0 + CHUNK:r0 + CHUNK + HALO, :],
        ], axis=0)
        acc = cb_ref[...] + cw_ref[mid:mid + 1, :] * blk[HALO:HALO + CHUNK, :]
        for k in range(CONV_K):
            if k != mid:
                rolled = pltpu.roll(blk, rows - (HALO - mid + k), axis=0)
                acc = acc + cw_ref[k:k + 1, :] * rolled[0:CHUNK, :]
        xc = _silu(acc)
        xcf_ref[r0:r0 + CHUNK, :] = xc
        xc_ref[r0:r0 + CHUNK, :] = xc.astype(BF16)

    def scan(c):
        r0 = c * CHUNK
        xc = xcf_ref[r0:r0 + CHUNK, :]
        y = _ssd_chunk(xc, dts[r0:r0 + CHUNK, :], dtb, a_row, tril_ref[...], colsel_ref[...],
                       headsel_ref[...], h_ref, reverse=False,
                       fresh=(t == 0) if c == 0 else None, fill=fill)
        yf_ref[r0:r0 + CHUNK, :] = y + dskip_ref[...] * xc[:, :D_SSM]

    def proj_z(lo):
        z_ref[:, lo:lo + MXU_N] = _dot(hn_ref[...], wzx_ref[:, lo:lo + MXU_N])

    def proj_u(lo):
        u_ref[:, lo:lo + MXU_N] = _dot(hn_ref[...], wu_ref[:, lo:lo + MXU_N]).astype(BF16)

    def proj_dt():
        dtr = _dot(hn_ref[...], wdt_ref[...])
        dt_ref[...] = dtr
        dt_ring[slot] = dtr

    def proj_xbc(lo):
        xbc_ring[slot, :, lo:lo + MXU_N] = _dot(
            hn_ref[...], wzx_ref[:, D_SSM + lo:D_SSM + lo + MXU_N])

    pieces = ([functools.partial(proj_z, lo) for lo in range(0, D_SSM, MXU_N)]
              + [functools.partial(proj_u, lo) for lo in range(0, D_POOL, MXU_N)]
              + [proj_dt]
              + [functools.partial(proj_xbc, lo) for lo in range(0, CONV_DIM, MXU_N)])
    queue = iter(pieces)

    def emit(n):
        for _ in range(n):
            piece = next(queue, None)
            if piece is not None:
                piece()

    plan = iter(FWD_FILL_PLAN * nchunk)

    def fill():
        emit(next(plan, 0))

    for c in range(nchunk):
        emit(FWD_FILL_CONV)
        conv(c)
    for c in range(nchunk):
        scan(c)
    emit(len(pieces))
    hn_next_ref[...] = (_rms(xn_ref[...]) * g_ref[...]).astype(BF16)


def _proj_fwd(x, g, wzx, wdt, wu, cw, cb, dtb, alog, dskip, tril, colsel, headsel,
              *, nb, seq, tm):
    nt = seq // tm
    ntot = nb * nt
    proj_row = lambda i: (jnp.minimum(i, ntot - 1), 0)
    norm_row = lambda i: (jnp.minimum(i + 1, ntot - 1), 0)
    scan_row = lambda i: (jnp.clip(i - FWD_LAG, 0, ntot - 1), 0)
    const = functools.partial(pl.BlockSpec, index_map=lambda i: (0, 0),
                              pipeline_mode=pl.Buffered(1))
    t = nb * seq
    return pl.pallas_call(
        functools.partial(_proj_fwd_kernel, tm=tm, nt=nt),
        grid=(ntot + FWD_LAG,),
        in_specs=[
            pl.BlockSpec((tm, D_MODEL), lambda i: (0, 0), pipeline_mode=pl.Buffered(1)),
            pl.BlockSpec((tm, D_MODEL), norm_row),
            const((1, D_MODEL)),
            const(wzx.shape),
            const((D_MODEL, DT_PAD)),
            const((D_MODEL, D_POOL)),
            const((SUBLANES, CONV_DIM)),
            const((1, CONV_DIM)),
            const((1, DT_PAD)),
            const((1, DT_PAD)),
            const((1, D_SSM)),
            const((CHUNK, DT_REP * CHUNK)),
            const((DT_PAD, N_HEADS * CHUNK)),
            const((DT_PAD, D_SSM)),
        ],
        out_specs=[
            pl.BlockSpec((tm, D_SSM), proj_row),
            pl.BlockSpec((tm, D_POOL), proj_row),
            pl.BlockSpec((tm, DT_PAD), proj_row),
            pl.BlockSpec((tm, CONV_DIM), scan_row),
            pl.BlockSpec((tm, D_SSM), scan_row),
        ],
        out_shape=[
            jax.ShapeDtypeStruct((t, D_SSM), F32),
            jax.ShapeDtypeStruct((t, D_POOL), BF16),
            jax.ShapeDtypeStruct((t, DT_PAD), F32),
            jax.ShapeDtypeStruct((t, CONV_DIM), BF16),
            jax.ShapeDtypeStruct((t, D_SSM), F32),
        ],
        scratch_shapes=[
            pltpu.VMEM((RING, tm, CONV_DIM), F32),
            pltpu.VMEM((RING, tm, DT_PAD), F32),
            pltpu.VMEM((GROUPS, D_STATE, GROUP_W), F32),
            pltpu.VMEM((tm, D_MODEL), BF16),
            pltpu.VMEM((tm, D_MODEL), BF16),
            pltpu.VMEM((tm, CONV_DIM), F32),
        ],
        compiler_params=pltpu.CompilerParams(
            dimension_semantics=("arbitrary",), vmem_limit_bytes=VMEM_LIMIT),
        name="proj_ssd_fwd",
    )(x, x, g, wzx, wdt, wu, cw, cb, dtb, alog, dskip, tril, colsel, headsel)


def _ssd_bwd_kernel(xc_ref, dt_ref, yf_ref, z_ref, u_ref, uprev_ref, unext_ref, x_ref,
                    dtb_ref, alog_ref, triu_ref, colsel_ref, headsel_ref, normw_ref,
                    dmat_ref, poolw_ref, pscale_ref, wout_ref, gpost_ref,
                    o_ref, y_ref, extu_ref, mixin_ring, mix_ref, h_ref, *, tm, nt, ntot):
    j = pl.program_id(0)
    tile = lax.rem(ntot - 1 - jnp.minimum(j, ntot - 1), nt)

    @pl.when(j == 0)
    def _():
        h_ref[...] = jnp.zeros_like(h_ref)
        mixin_ring[...] = jnp.zeros_like(mixin_ring)

    cur = lax.rem(j, 2)
    prv = 1 - cur

    def out_proj(lo):
        mix_ref[:, lo:lo + MXU_N] = _dot(mixin_ring[prv], wout_ref[:, lo:lo + MXU_N])

    pieces = [functools.partial(out_proj, lo) for lo in range(0, D_MODEL, MXU_N)]
    queue = iter(pieces)

    def emit(n):
        for _ in range(n):
            piece = next(queue, None)
            if piece is not None:
                piece()

    nchunk = tm // CHUNK
    plan = iter(BWD_FILL_PLAN * nchunk)

    def fill():
        emit(next(plan, 0))

    a_row = _dt_consts(alog_ref)
    dtb = dtb_ref[...]
    for c in reversed(range(nchunk)):
        r0 = c * CHUNK
        y = _ssd_chunk(xc_ref[r0:r0 + CHUNK, :], dt_ref[r0:r0 + CHUNK, :], dtb, a_row,
                       triu_ref[...], colsel_ref[...], headsel_ref[...], h_ref, reverse=True,
                       fresh=(tile == nt - 1) if c == nchunk - 1 else None, fill=fill)
        y_ref[r0:r0 + CHUNK, :] = y + yf_ref[r0:r0 + CHUNK, :]
    emit(len(pieces))

    v = y_ref[...] * _silu(z_ref[...])
    for g in range(GROUPS):
        lo = g * GROUP_W
        mixin_ring[cur, :, lo:lo + GROUP_W] = (_rms(v[:, lo:lo + GROUP_W])
                                               * normw_ref[:, lo:lo + GROUP_W]).astype(BF16)

    ph = POOL_HALO
    extu_ref[0:ph, :] = jnp.where(tile > 0, uprev_ref[...], jnp.zeros_like(uprev_ref))
    extu_ref[ph:ph + tm, :] = u_ref[...]
    extu_ref[ph + tm:2 * ph + tm, :] = jnp.where(tile < nt - 1, unext_ref[...],
                                                 jnp.zeros_like(unext_ref))
    extu_ref[2 * ph + tm:, :] = jnp.zeros((POOL_K - CHUNK - 2 * ph, D_POOL), BF16)
    for c in range(nchunk):
        r0 = c * CHUNK
        first = jnp.logical_and(tile == 0, c == 0)
        final = jnp.logical_and(tile == nt - 1, c == nchunk - 1)
        variant = jnp.where(first, 0, jnp.where(final, 2, 1))
        for gi in range(len(POOL_WINDOWS)):
            lo = gi * POOL_GROUP_DIM
            band = dmat_ref[variant * len(POOL_WINDOWS) + gi]
            d = _dot(band, extu_ref[r0:r0 + POOL_K, lo:lo + POOL_GROUP_DIM]).astype(BF16)
            pooled = _dot(d, poolw_ref[gi]) * pscale_ref[:, lo:lo + POOL_GROUP_DIM]
            mixin_ring[cur, r0:r0 + CHUNK,
                       D_SSM + lo:D_SSM + lo + POOL_GROUP_DIM] = pooled.astype(BF16)

    o_ref[...] = x_ref[...] + _rms(mix_ref[...]) * gpost_ref[...]


def _ssd_bwd(xc, dt, yf, z, u, x, dtb, alog, triu, colsel, headsel, normw, dmat, poolw, pscale,
             wout, gpost, *, nb, seq, tm):
    nt = seq // tm
    hb = tm // POOL_HALO
    nhalo = nb * seq // POOL_HALO
    ntot = nb * nt
    tile = lambda j: ntot - 1 - jnp.minimum(j, ntot - 1)
    row = lambda j: (tile(j), 0)
    prev = lambda j: (jnp.maximum(tile(j) * hb - 1, 0), 0)
    nxt = lambda j: (jnp.minimum((tile(j) + 1) * hb, nhalo - 1), 0)
    out_row = lambda j: (jnp.minimum(ntot - j, ntot - 1), 0)
    const2 = functools.partial(pl.BlockSpec, index_map=lambda j: (0, 0),
                               pipeline_mode=pl.Buffered(1))
    const3 = functools.partial(pl.BlockSpec, index_map=lambda j: (0, 0, 0),
                               pipeline_mode=pl.Buffered(1))
    t = nb * seq
    nvar = 3 * len(POOL_WINDOWS)
    return pl.pallas_call(
        functools.partial(_ssd_bwd_kernel, tm=tm, nt=nt, ntot=ntot),
        grid=(ntot + 1,),
        in_specs=[
            pl.BlockSpec((tm, CONV_DIM), row),
            pl.BlockSpec((tm, DT_PAD), row),
            pl.BlockSpec((tm, D_SSM), row),
            pl.BlockSpec((tm, D_SSM), row),
            pl.BlockSpec((tm, D_POOL), row),
            pl.BlockSpec((POOL_HALO, D_POOL), prev),
            pl.BlockSpec((POOL_HALO, D_POOL), nxt),
            pl.BlockSpec((tm, D_MODEL), out_row),
            const2((1, DT_PAD)),
            const2((1, DT_PAD)),
            const2((CHUNK, DT_REP * CHUNK)),
            const2((DT_PAD, N_HEADS * CHUNK)),
            const2((DT_PAD, D_SSM)),
            const2((1, D_SSM)),
            const3((nvar, CHUNK, POOL_K)),
            const3((len(POOL_WINDOWS), POOL_GROUP_DIM, POOL_GROUP_DIM)),
            const2((1, D_POOL)),
            const2((D_SSM + D_POOL, D_MODEL)),
            const2((1, D_MODEL)),
        ],
        out_specs=pl.BlockSpec((tm, D_MODEL), out_row),
        out_shape=jax.ShapeDtypeStruct((t, D_MODEL), F32),
        scratch_shapes=[
            pltpu.VMEM((tm, D_SSM), F32),
            pltpu.VMEM((tm + CHUNK, D_POOL), BF16),
            pltpu.VMEM((2, tm, D_SSM + D_POOL), BF16),
            pltpu.VMEM((tm, D_MODEL), F32),
            pltpu.VMEM((GROUPS, D_STATE, GROUP_W), F32),
        ],
        compiler_params=pltpu.CompilerParams(
            dimension_semantics=("arbitrary",), vmem_limit_bytes=VMEM_LIMIT),
        name="ssd_bwd_mix",
    )(xc, dt, yf, z, u, u, u, x, dtb, alog, triu, colsel, headsel, normw, dmat, poolw, pscale,
      wout, gpost)


def _mlp_kernel(x_ref, gpre_ref, w1_ref, w2_ref, gpost_ref, o_ref, h_ref, acc_ref, act_ref):
    f = pl.program_id(1)
    last = pl.num_programs(1) - 1
    blocks = [slice(r, r + MLP_ROWS) for r in range(0, x_ref.shape[0], MLP_ROWS)]

    def sq_relu(a):
        a = jnp.maximum(a, 0.0)
        return (a * a).astype(BF16)

    @pl.when(f == 0)
    def _():
        for rs in blocks:
            h_ref[rs, :] = (_rms(x_ref[rs, :]) * gpre_ref[...]).astype(BF16)
            act_ref[rs, :] = sq_relu(_dot(h_ref[rs, :], w1_ref[...]))
        acc_ref[...] = _dot(act_ref[...], w2_ref[...])

    @pl.when(jnp.logical_and(f > 0, f < last))
    def _():
        acc_ref[...] += _dot(sq_relu(_dot(h_ref[...], w1_ref[...])), w2_ref[...])

    @pl.when(f == last)
    def _():
        act_ref[...] = sq_relu(_dot(h_ref[...], w1_ref[...]))
        for rs in blocks:
            m = acc_ref[rs, :] + _dot(act_ref[rs, :], w2_ref[...])
            o_ref[rs, :] = x_ref[rs, :] + _rms(m) * gpost_ref[...]


def _mlp(x, gpre, w1, w2, gpost, *, tm, tf):
    t = x.shape[0]
    row = lambda i, f: (i, 0)
    fixed = lambda i, f: (0, 0)
    const = functools.partial(pl.BlockSpec, index_map=fixed, pipeline_mode=pl.Buffered(1))
    return pl.pallas_call(
        _mlp_kernel,
        grid=(t // tm, D_FF // tf),
        in_specs=[
            pl.BlockSpec((tm, D_MODEL), row),
            const((1, D_MODEL)),
            pl.BlockSpec((D_MODEL, tf), lambda i, f: (0, f)),
            pl.BlockSpec((tf, D_MODEL), lambda i, f: (f, 0)),
            const((1, D_MODEL)),
        ],
        out_specs=pl.BlockSpec((tm, D_MODEL), row),
        out_shape=jax.ShapeDtypeStruct((t, D_MODEL), F32),
        scratch_shapes=[
            pltpu.VMEM((tm, D_MODEL), BF16),
            pltpu.VMEM((tm, D_MODEL), F32),
            pltpu.VMEM((tm, tf), BF16),
        ],
        compiler_params=pltpu.CompilerParams(
            dimension_semantics=("arbitrary", "arbitrary"), vmem_limit_bytes=VMEM_LIMIT),
        name="mlp",
    )(x, gpre, w1, w2, gpost)


def _selectors():
    q = np.arange(CHUNK)[:, None]
    s = np.arange(CHUNK)[None, :]
    tril = np.tile((s <= q).astype(np.float32), (1, DT_REP))
    triu = np.tile((s >= q).astype(np.float32), (1, DT_REP))
    colsel = np.zeros((2, DT_PAD, N_HEADS * CHUNK), np.float32)
    headsel = np.zeros((2, DT_PAD, D_SSM), np.float32)
    for d in range(2):
        for h in range(N_HEADS):
            for rep in range(DT_REP):
                src = rep * DT_COLS + d * N_HEADS + h
                colsel[d, src, h * CHUNK:(h + 1) * CHUNK] = 1.0
                headsel[d, src, h * HEAD_DIM:(h + 1) * HEAD_DIM] = 1.0
    as_bf16 = lambda a: jnp.asarray(a, dtype=BF16)
    return as_bf16(tril), as_bf16(triu), as_bf16(colsel), as_bf16(headsel)


def _pool_bands(seq):
    out = np.zeros((3, len(POOL_WINDOWS), CHUNK, POOL_K), np.float32)
    for v, start in enumerate((0, CHUNK, seq - CHUNK)):
        t = start + np.arange(CHUNK)[:, None]
        src = start + np.arange(POOL_K)[None, :] - POOL_HALO
        for gi, w in enumerate(POOL_WINDOWS):
            hi = np.minimum(t + w // 2, seq)
            lo = np.maximum(t - w // 2, 0)
            inside = (src >= lo) & (src < hi)
            out[v, gi] = inside / (hi - lo).astype(np.float32) - (src == t)
    return jnp.asarray(out.reshape(3 * len(POOL_WINDOWS), CHUNK, POOL_K), dtype=BF16)


def _layer_params(i, g_mix_pre, g_mix_post, g_mlp_pre, g_mlp_post, w_in, conv_w, conv_b,
                  dt_bias, a_log, d_skip, ssm_norm_w, pool_w, pool_scale, w_out,
                  w_mlp_in, w_mlp_out):
    row = lambda v: v.reshape(1, -1).astype(F32)
    rep_lanes = lambda v: jnp.pad(jnp.tile(v.astype(F32).reshape(-1, DT_COLS), (1, DT_REP)),
                                  ((0, 0), (0, DT_PAD - DT_REP * DT_COLS)))
    o_dt = D_SSM + CONV_DIM
    w = w_in[i]
    return dict(
        g_mix_pre=row(g_mix_pre[i]), g_mix_post=row(g_mix_post[i]),
        g_mlp_pre=row(g_mlp_pre[i]), g_mlp_post=row(g_mlp_post[i]),
        wzx=w_in.astype(BF16)[i],
        wdt=rep_lanes(w[:, o_dt:o_dt + DT_COLS]).astype(BF16),
        wu=w[:, o_dt + DT_COLS:].astype(BF16),
        cw=jnp.pad(conv_w[i].astype(F32), ((0, SUBLANES - CONV_K), (0, 0))),
        cb=row(conv_b[i]),
        dtb=rep_lanes(dt_bias[i]), alog=rep_lanes(a_log[i]),
        dskip=row(jnp.repeat(d_skip[i], HEAD_DIM)),
        normw=row(ssm_norm_w[i]),
        poolw=pool_w[i].astype(BF16), pscale=row(pool_scale[i]),
        wout=w_out[i].astype(BF16),
        w1=w_mlp_in[i].astype(BF16), w2=w_mlp_out[i].astype(BF16),
    )


def _trunk(x, layers, consts, *, tm_ssd, tm_mlp):
    nb, seq, _ = x.shape
    tril, triu, colsel, headsel = consts
    dmat = _pool_bands(seq)
    x = x.reshape(nb * seq, D_MODEL)
    for p in layers:
        z, u, dt, xc, yf = _proj_fwd(x, p["g_mix_pre"], p["wzx"], p["wdt"], p["wu"],
                                     p["cw"], p["cb"], p["dtb"], p["alog"], p["dskip"],
                                     tril, colsel[0], headsel[0], nb=nb, seq=seq, tm=tm_ssd)
        x = _ssd_bwd(xc, dt, yf, z, u, x, p["dtb"], p["alog"], triu, colsel[1], headsel[1],
                     p["normw"], dmat, p["poolw"], p["pscale"], p["wout"], p["g_mix_post"],
                     nb=nb, seq=seq, tm=tm_ssd)
        x = _mlp(x, p["g_mlp_pre"], p["w1"], p["w2"], p["g_mlp_post"], tm=tm_mlp, tf=MLP_TF)
    return x.reshape(nb, seq, D_MODEL)


def kernel(x_prompt, x_sample, g_mix_pre, g_mix_post, g_mlp_pre, g_mlp_post, w_in, conv_w, conv_b, dt_bias, a_log, d_skip, ssm_norm_w, pool_w, pool_scale, w_out, w_mlp_in, w_mlp_out):
    depth = w_in.shape[0]
    layers = [_layer_params(i, g_mix_pre, g_mix_post, g_mlp_pre, g_mlp_post, w_in, conv_w,
                            conv_b, dt_bias, a_log, d_skip, ssm_norm_w, pool_w, pool_scale,
                            w_out, w_mlp_in, w_mlp_out) for i in range(depth)]
    consts = _selectors()
    run = functools.partial(_trunk, layers=layers, consts=consts,
                            tm_ssd=SSD_TM, tm_mlp=MLP_TM)
    return (run(x_prompt), run(x_sample))
```
